```python
import jax, jax.numpy as jnp
from jax import lax
import numpy as np

D_MODEL = 4096
BATCH = 2
SEQ = 8192
DEPTH = 2

POOL_WINDOWS = (2, 4, 8, 16)
POOL_GROUPS = len(POOL_WINDOWS)
POOL_WIDTH = D_MODEL // 2
POOL_GROUP_DIM = POOL_WIDTH // POOL_GROUPS
SGU_WIDTH = D_MODEL // 2
SGU_HEADS = 8
SGU_HEAD_DIM = SGU_WIDTH // SGU_HEADS
SGU_CHUNK = 128
CONV_CHANNELS = D_MODEL
CONV_TAPS = 31
N_EXPERTS = 64
TOP_K = 6
N_EXPERT_GROUPS = 8
TOPK_GROUPS = 4
EXPERT_DIM = D_MODEL // 8
SHARED_DIM = D_MODEL // 8
ROUTED_SCALE = 2.5
EXPERT_BLOCK = 256
LN_EPS = 1e-5
DEEPNORM_ALPHA = (2 * DEPTH) ** 0.25
DEEPNORM_BETA = (8 * DEPTH) ** -0.25
N_EVEN = (DEPTH + 1) // 2
N_ODD = DEPTH // 2

kernel_name = 'hybrid_pool_sgu_conformer_moe_deepnorm'


def layer_norm(x, g, b):
    x32 = x.astype(jnp.float32)
    mu = jnp.mean(x32, axis=-1, keepdims=True)
    xc = x32 - mu
    var = jnp.mean(xc * xc, axis=-1, keepdims=True)
    return (xc * lax.rsqrt(var + LN_EPS) * g.astype(jnp.float32) + b.astype(jnp.float32)).astype(x.dtype)


def causal_window_mean(c0, w, seq):
    wsum = jnp.concatenate([c0[:, 1:w], c0[:, w:] - c0[:, : seq + 1 - w]], axis=1)
    count = jnp.minimum(jnp.arange(1, seq + 1, dtype=jnp.float32), float(w))
    return wsum / count[None, :, None]


def pool_sgu_mixer(x, w_in, w_pool, pool_scale, sgu_ln_g, sgu_ln_b, sgu_w, sgu_b, w_out):
    bsz, seq, _ = x.shape
    proj = x @ w_in
    a = proj[..., :POOL_WIDTH]
    u = jax.nn.gelu(proj[..., POOL_WIDTH:POOL_WIDTH + SGU_WIDTH])
    v = jax.nn.gelu(proj[..., POOL_WIDTH + SGU_WIDTH:])
    a32 = a.astype(jnp.float32).reshape(bsz, seq, POOL_GROUPS, POOL_GROUP_DIM)
    c0 = jnp.concatenate([jnp.zeros_like(a32[:, :1]), jnp.cumsum(a32, axis=1)], axis=1)
    pooled = jnp.stack([causal_window_mean(c0[:, :, g], w, seq) for g, w in enumerate(POOL_WINDOWS)], axis=2) - a32
    a_out = jnp.einsum('bsgc,gcd->bsgd', pooled.astype(x.dtype), w_pool).reshape(bsz, seq, POOL_WIDTH) * pool_scale
    v = v.reshape(bsz, seq // SGU_CHUNK, SGU_CHUNK, SGU_HEADS, SGU_HEAD_DIM)
    v = layer_norm(v, sgu_ln_g.reshape(SGU_HEADS, SGU_HEAD_DIM), sgu_ln_b.reshape(SGU_HEADS, SGU_HEAD_DIM))
    causal = jnp.tril(jnp.ones((SGU_CHUNK, SGU_CHUNK), dtype=bool))
    w_s = jnp.where(causal[None], sgu_w, jnp.zeros_like(sgu_w))
    v = jnp.einsum('hts,bnshc->bnthc', w_s, v) + sgu_b.T[:, :, None]
    b_out = u * v.reshape(bsz, seq, SGU_WIDTH)
    return jnp.concatenate([a_out, b_out], axis=-1) @ w_out


def conformer_conv_mixer(x, w_pw1, b_pw1, dw_w, dw_b, ln_g, ln_b, w_pw2, b_pw2):
    h = x @ w_pw1 + b_pw1
    h = h[..., :CONV_CHANNELS] * jax.nn.sigmoid(h[..., CONV_CHANNELS:])
    h = lax.conv_general_dilated(
        h, dw_w[:, None, :], window_strides=(1,), padding=[(CONV_TAPS - 1, 0)],
        dimension_numbers=('NWC', 'WIO', 'NWC'), feature_group_count=CONV_CHANNELS) + dw_b
    h = jax.nn.silu(layer_norm(h, ln_g, ln_b))
    return h @ w_pw2 + b_pw2


def route(h2d, w_router, b_router):
    n_tok = h2d.shape[0]
    scores = jax.nn.sigmoid((h2d @ w_router).astype(jnp.float32))
    choice = scores + b_router.astype(jnp.float32)
    grouped = choice.reshape(n_tok, N_EXPERT_GROUPS, N_EXPERTS // N_EXPERT_GROUPS)
    group_score = jnp.sum(lax.top_k(grouped, 2)[0], axis=-1)
    _, top_groups = lax.top_k(group_score, TOPK_GROUPS)
    group_mask = jnp.sum(jax.nn.one_hot(top_groups, N_EXPERT_GROUPS, dtype=jnp.float32), axis=1) > 0
    expert_mask = jnp.repeat(group_mask, N_EXPERTS // N_EXPERT_GROUPS, axis=1)
    _, top_e = lax.top_k(jnp.where(expert_mask, choice, -jnp.inf), TOP_K)
    gate = jnp.take_along_axis(scores, top_e, axis=1)
    gate = gate / jnp.sum(gate, axis=-1, keepdims=True) * ROUTED_SCALE
    return top_e, gate


def moe_ffn(h2d, layer, w_router, b_router, w_gu, w_down, w_shared_gu, w_shared_down):
    n_tok, d = h2d.shape
    top_e, gate = route(h2d, w_router[layer], b_router[layer])
    n_assign = n_tok * TOP_K
    flat_e = top_e.reshape(-1)
    flat_tok = jnp.arange(n_assign, dtype=jnp.int32) // TOP_K
    order = jnp.argsort(flat_e)
    sorted_e = flat_e[order]
    counts = jnp.bincount(flat_e, length=N_EXPERTS)
    starts = jnp.cumsum(counts) - counts
    padded = (counts + EXPERT_BLOCK - 1) // EXPERT_BLOCK * EXPERT_BLOCK
    pad_ends = jnp.cumsum(padded)
    pad_starts = pad_ends - padded
    dest = pad_starts[sorted_e] + jnp.arange(n_assign) - starts[sorted_e]
    n_blocks = -(-n_assign // EXPERT_BLOCK) + N_EXPERTS
    n_rows = n_blocks * EXPERT_BLOCK
    row_tok = jnp.full((n_rows,), n_tok, jnp.int32).at[dest].set(flat_tok[order])
    row_gate = jnp.zeros((n_rows,), jnp.float32).at[dest].set(gate.reshape(-1)[order])
    block_start = jnp.arange(n_blocks) * EXPERT_BLOCK
    block_e = jnp.minimum(jnp.sum(pad_ends[None, :] <= block_start[:, None], axis=1), N_EXPERTS - 1)
    x_pad = jnp.concatenate([h2d, jnp.zeros((1, d), h2d.dtype)], axis=0)

    def expert_block(acc, blk):
        tok, g, e = blk
        gt, up = jnp.split(x_pad[tok] @ w_gu[layer, e], 2, axis=-1)
        y = (jax.nn.silu(gt) * up) @ w_down[layer, e]
        return acc.at[tok].add(y * g[:, None].astype(y.dtype)), None

    acc, _ = lax.scan(expert_block, jnp.zeros((n_tok + 1, d), h2d.dtype),
                      (row_tok.reshape(n_blocks, EXPERT_BLOCK), row_gate.reshape(n_blocks, EXPERT_BLOCK), block_e))
    sg, su = jnp.split(h2d @ w_shared_gu[layer], 2, axis=-1)
    shared = (jax.nn.silu(sg) * su) @ w_shared_down[layer]
    return acc[:n_tok] + shared


def setup_inputs(seed: int = 0) -> dict:
    key = jax.random.key(seed)
    ks = jax.random.split(key, 27)
    f32 = jnp.float32

    def nrm(k, shape, scale):
        return jax.random.normal(k, shape, f32) * scale

    def gain(k, shape):
        return 1.0 + 0.02 * jax.random.normal(k, shape, f32)

    mix_in = POOL_WIDTH + 2 * SGU_WIDTH
    mix_out = POOL_WIDTH + SGU_WIDTH
    return {
        'x': nrm(ks[0], (BATCH, SEQ, D_MODEL), 1.0),
        'w_in_ab': nrm(ks[1], (N_EVEN, D_MODEL, mix_in), D_MODEL ** -0.5),
        'w_pool': nrm(ks[2], (N_EVEN, POOL_GROUPS, POOL_GROUP_DIM, POOL_GROUP_DIM), POOL_GROUP_DIM ** -0.5),
        'pool_scale': gain(ks[3], (N_EVEN, POOL_WIDTH)),
        'sgu_ln_g': gain(ks[4], (N_EVEN, SGU_WIDTH)),
        'sgu_ln_b': nrm(ks[5], (N_EVEN, SGU_WIDTH), 0.02),
        'sgu_w': nrm(ks[6], (N_EVEN, SGU_HEADS, SGU_CHUNK, SGU_CHUNK), SGU_CHUNK ** -0.5),
        'sgu_b': gain(ks[7], (N_EVEN, SGU_HEADS, SGU_CHUNK)),
        'w_out_ab': nrm(ks[8], (N_EVEN, mix_out, D_MODEL), DEEPNORM_BETA * mix_out ** -0.5),
        'w_pw1': nrm(ks[9], (N_ODD, D_MODEL, 2 * CONV_CHANNELS), D_MODEL ** -0.5),
        'b_pw1': nrm(ks[10], (N_ODD, 2 * CONV_CHANNELS), 0.02),
        'dw_w': nrm(ks[11], (N_ODD, CONV_TAPS, CONV_CHANNELS), CONV_TAPS ** -0.5),
        'dw_b': nrm(ks[12], (N_ODD, CONV_CHANNELS), 0.02),
        'conv_ln_g': gain(ks[13], (N_ODD, CONV_CHANNELS)),
        'conv_ln_b': nrm(ks[14], (N_ODD, CONV_CHANNELS), 0.02),
        'w_pw2': nrm(ks[15], (N_ODD, CONV_CHANNELS, D_MODEL), DEEPNORM_BETA * CONV_CHANNELS ** -0.5),
        'b_pw2': nrm(ks[16], (N_ODD, D_MODEL), 0.02),
        'mix_ln_g': gain(ks[17], (DEPTH, D_MODEL)),
        'mix_ln_b': nrm(ks[18], (DEPTH, D_MODEL), 0.02),
        'w_router': nrm(ks[19], (DEPTH, D_MODEL, N_EXPERTS), D_MODEL ** -0.5),
        'b_router': nrm(ks[20], (DEPTH, N_EXPERTS), 0.01),
        'w_gu': nrm(ks[21], (DEPTH, N_EXPERTS, D_MODEL, 2 * EXPERT_DIM), D_MODEL ** -0.5),
        'w_down': nrm(ks[22], (DEPTH, N_EXPERTS, EXPERT_DIM, D_MODEL), DEEPNORM_BETA * EXPERT_DIM ** -0.5),
        'w_shared_gu': nrm(ks[23], (DEPTH, D_MODEL, 2 * SHARED_DIM), D_MODEL ** -0.5),
        'w_shared_down': nrm(ks[24], (DEPTH, SHARED_DIM, D_MODEL), DEEPNORM_BETA * SHARED_DIM ** -0.5),
        'ffn_ln_g': gain(ks[25], (DEPTH, D_MODEL)),
        'ffn_ln_b': nrm(ks[26], (DEPTH, D_MODEL), 0.02),
    }


def reference(x, w_in_ab, w_pool, pool_scale, sgu_ln_g, sgu_ln_b, sgu_w, sgu_b, w_out_ab,
              w_pw1, b_pw1, dw_w, dw_b, conv_ln_g, conv_ln_b, w_pw2, b_pw2,
              mix_ln_g, mix_ln_b, w_router, b_router, w_gu, w_down, w_shared_gu, w_shared_down,
              ffn_ln_g, ffn_ln_b):
    h = x
    for layer in range(DEPTH):
        i = layer // 2
        if layer % 2 == 0:
            mix = pool_sgu_mixer(h, w_in_ab[i], w_pool[i], pool_scale[i], sgu_ln_g[i], sgu_ln_b[i],
                                 sgu_w[i], sgu_b[i], w_out_ab[i])
        else:
            mix = conformer_conv_mixer(h, w_pw1[i], b_pw1[i], dw_w[i], dw_b[i], conv_ln_g[i], conv_ln_b[i],
                                       w_pw2[i], b_pw2[i])
        h = layer_norm(DEEPNORM_ALPHA * h + mix, mix_ln_g[layer], mix_ln_b[layer])
        ffn = moe_ffn(h.reshape(-1, D_MODEL), layer, w_router, b_router, w_gu, w_down,
                      w_shared_gu, w_shared_down).reshape(h.shape)
        h = layer_norm(DEEPNORM_ALPHA * h + ffn, ffn_ln_g[layer], ffn_ln_b[layer])
    return h
```

```python
import functools

import jax
import jax.numpy as jnp
from jax import lax
from jax.experimental import pallas as pl
from jax.experimental.pallas import tpu as pltpu

F32 = jnp.float32
BF16 = jnp.bfloat16

POOL_WINDOWS = (2, 4, 8, 16)
POOL_HALO = 16
SGU_HEADS = 8
SGU_CHUNK = 128
CONV_TAPS = 31
CONV_HALO = 32
N_EXPERTS = 64
TOP_K = 6
N_EXPERT_GROUPS = 8
GROUP_SIZE = N_EXPERTS // N_EXPERT_GROUPS
TOPK_GROUPS = 4
ROUTED_SCALE = 2.5
EXPERT_BLOCK = 256
LN_EPS = 1e-5

LANES = 128
SUBLANES = 8
VMEM_LIMIT_BYTES = 56 * 1024 * 1024


def _params(*semantics):
    return pltpu.CompilerParams(dimension_semantics=semantics, vmem_limit_bytes=VMEM_LIMIT_BYTES)


def _tile(n, pref):
    t = min(n, pref)
    while n % t:
        t //= 2
    return t


def _layer_norm_rows(z, g, b):
    mu = jnp.mean(z, axis=-1, keepdims=True)
    zc = z - mu
    var = jnp.mean(zc * zc, axis=-1, keepdims=True)
    return zc * lax.rsqrt(var + LN_EPS) * g + b


def _mm_kernel(a_ref, b_ref, o_ref):
    o_ref[...] = jnp.dot(a_ref[...], b_ref[...], preferred_element_type=F32).astype(o_ref.dtype)


def _mm_bias_kernel(a_ref, b_ref, bias_ref, o_ref):
    acc = jnp.dot(a_ref[...], b_ref[...], preferred_element_type=F32)
    o_ref[...] = (acc + bias_ref[...]).astype(o_ref.dtype)


def _matmul(a, b, bias=None, out_dtype=F32):
    m, k = a.shape
    n = b.shape[1]
    tm, tn = _tile(m, 512), _tile(n, 1024)
    in_specs = [pl.BlockSpec((tm, k), lambda j, i: (i, 0)),
                pl.BlockSpec((k, tn), lambda j, i: (0, j))]
    args = [a, b]
    body = _mm_kernel
    if bias is not None:
        in_specs.append(pl.BlockSpec((1, tn), lambda j, i: (0, j)))
        args.append(bias.reshape(1, n))
        body = _mm_bias_kernel
    return pl.pallas_call(
        body,
        grid=(n // tn, m // tm),
        in_specs=in_specs,
        out_specs=pl.BlockSpec((tm, tn), lambda j, i: (i, j)),
        out_shape=jax.ShapeDtypeStruct((m, n), out_dtype),
        compiler_params=_params("parallel", "parallel"),
        name="dense_matmul",
    )(*args)


def _ln_kernel(h_ref, m_ref, g_ref, b_ref, o32_ref, o16_ref, *, alpha):
    y = _layer_norm_rows(alpha * h_ref[...] + m_ref[...], g_ref[...], b_ref[...])
    o32_ref[...] = y
    o16_ref[...] = y.astype(BF16)


def _residual_ln(h, mix, g, b, alpha):
    t, d = h.shape
    tm = _tile(t, 256)
    row = pl.BlockSpec((tm, d), lambda i: (i, 0))
    vec = pl.BlockSpec((1, d), lambda i: (0, 0))
    return pl.pallas_call(
        functools.partial(_ln_kernel, alpha=alpha),
        grid=(t // tm,),
        in_specs=[row, row, vec, vec],
        out_specs=[row, row],
        out_shape=[jax.ShapeDtypeStruct((t, d), F32), jax.ShapeDtypeStruct((t, d), BF16)],
        compiler_params=_params("parallel"),
        name="residual_ln",
    )(h, mix, g.reshape(1, d), b.reshape(1, d))


def _mix0_kernel(pa_ref, pv_ref, halo_ref, wp_ref, ps_ref, lg_ref, lb_ref, sw_ref, sb_ref,
                 o_ref, ext_ref, *, tm, seq_tiles, head_dim):
    i = pl.program_id(0)
    j = pl.program_id(1)
    n_pool = len(POOL_WINDOWS)
    seq_tile = i % seq_tiles

    for grp, window in enumerate(POOL_WINDOWS):
        @pl.when(j == grp)
        def _(window=window):
            a = pa_ref[...]
            halo = halo_ref[...]
            ext_ref[0:POOL_HALO, :] = jnp.where(seq_tile == 0, jnp.zeros_like(halo), halo)
            ext_ref[POOL_HALO:POOL_HALO + tm, :] = a
            wsum = a
            for back in range(1, window):
                wsum = wsum + ext_ref[pl.ds(POOL_HALO - back, tm), :]
            pos = seq_tile * tm + lax.broadcasted_iota(jnp.int32, (tm, 1), 0)
            count = jnp.minimum(pos + 1, window).astype(F32)
            pooled = wsum / count - a
            mapped = jnp.dot(pooled.astype(BF16), wp_ref[...], preferred_element_type=F32)
            o_ref[...] = (mapped * ps_ref[...]).astype(o_ref.dtype)

    @pl.when(j >= n_pool)
    def _():
        u = jax.nn.gelu(pa_ref[...])
        v = jax.nn.gelu(pv_ref[...])
        r = lax.broadcasted_iota(jnp.int32, (SGU_CHUNK, SGU_CHUNK), 0)
        c = lax.broadcasted_iota(jnp.int32, (SGU_CHUNK, SGU_CHUNK), 1)
        for hh in range(pa_ref.shape[1] // head_dim):
            cols = slice(hh * head_dim, (hh + 1) * head_dim)
            vn = _layer_norm_rows(v[:, cols], lg_ref[:, cols], lb_ref[:, cols]).astype(BF16)
            w_s = jnp.where(r >= c, sw_ref[hh], 0.0).astype(BF16)
            bias = sb_ref[hh]
            for ch in range(tm // SGU_CHUNK):
                rows = slice(ch * SGU_CHUNK, (ch + 1) * SGU_CHUNK)
                sv = jnp.dot(w_s, vn[rows], preferred_element_type=F32) + bias
                o_ref[rows, cols] = (u[rows, cols] * sv).astype(o_ref.dtype)


def _pool_sgu_mix(proj, w_pool, pool_scale, sgu_ln_g, sgu_ln_b, sgu_w, sgu_b, seq):
    t = proj.shape[0]
    n_pool = len(POOL_WINDOWS)
    gd = w_pool.shape[-1]
    pool_w = n_pool * gd
    sgu_w_total = sgu_ln_g.shape[-1]
    head_dim = sgu_w_total // SGU_HEADS
    hps = gd // head_dim
    n_sgu = sgu_w_total // gd
    tm = _tile(seq, 512)
    seq_tiles = seq // tm
    v0 = (pool_w + sgu_w_total) // gd
    halo_per_tile = tm // POOL_HALO

    kern = functools.partial(_mix0_kernel, tm=tm, seq_tiles=seq_tiles, head_dim=head_dim)
    return pl.pallas_call(
        kern,
        grid=(t // tm, n_pool + n_sgu),
        in_specs=[
            pl.BlockSpec((tm, gd), lambda i, j: (i, j)),
            pl.BlockSpec((tm, gd), lambda i, j: (i, jnp.maximum(j - n_pool, 0) + v0)),
            pl.BlockSpec((POOL_HALO, gd),
                         lambda i, j: (jnp.maximum(i * halo_per_tile - 1, 0), jnp.minimum(j, n_pool - 1))),
            pl.BlockSpec((None, gd, gd), lambda i, j: (jnp.minimum(j, n_pool - 1), 0, 0)),
            pl.BlockSpec((1, gd), lambda i, j: (0, jnp.minimum(j, n_pool - 1))),
            pl.BlockSpec((1, gd), lambda i, j: (0, jnp.maximum(j - n_pool, 0))),
            pl.BlockSpec((1, gd), lambda i, j: (0, jnp.maximum(j - n_pool, 0))),
            pl.BlockSpec((hps, SGU_CHUNK, SGU_CHUNK), lambda i, j: (jnp.maximum(j - n_pool, 0), 0, 0)),
            pl.BlockSpec((hps, SGU_CHUNK, 1), lambda i, j: (jnp.maximum(j - n_pool, 0), 0, 0)),
        ],
        out_specs=pl.BlockSpec((tm, gd), lambda i, j: (i, j)),
        out_shape=jax.ShapeDtypeStruct((t, pool_w + sgu_w_total), BF16),
        scratch_shapes=[pltpu.VMEM((POOL_HALO + tm, gd), F32)],
        compiler_params=_params("parallel", "arbitrary"),
        name="pool_sgu_mix",
    )(proj, proj, proj, w_pool.astype(BF16), pool_scale.reshape(1, pool_w),
      sgu_ln_g.reshape(1, -1), sgu_ln_b.reshape(1, -1), sgu_w, sgu_b[:, :, None])


def _glu_kernel(a_ref, bv_ref, bg_ref, biasv_ref, biasg_ref, o_ref):
    a = a_ref[...]
    val = jnp.dot(a, bv_ref[...], preferred_element_type=F32) + biasv_ref[...]
    gate = jnp.dot(a, bg_ref[...], preferred_element_type=F32) + biasg_ref[...]
    res = val * jax.nn.sigmoid(gate)
    for lc in range(o_ref.shape[0]):
        o_ref[lc] = res[:, lc * LANES:(lc + 1) * LANES]


def _pointwise_glu(a, w, bias):
    m, k = a.shape
    c = w.shape[1] // 2
    tm, tn = _tile(m, 512), _tile(c, 512)
    gate0 = c // tn
    bias2 = bias.reshape(1, 2 * c)
    return pl.pallas_call(
        _glu_kernel,
        grid=(c // tn, m // tm),
        in_specs=[
            pl.BlockSpec((tm, k), lambda j, i: (i, 0)),
            pl.BlockSpec((k, tn), lambda j, i: (0, j)),
            pl.BlockSpec((k, tn), lambda j, i: (0, j + gate0)),
            pl.BlockSpec((1, tn), lambda j, i: (0, j)),
            pl.BlockSpec((1, tn), lambda j, i: (0, j + gate0)),
        ],
        out_specs=pl.BlockSpec((tn // LANES, tm, LANES), lambda j, i: (j, i, 0)),
        out_shape=jax.ShapeDtypeStruct((c // LANES, m, LANES), F32),
        compiler_params=_params("parallel", "parallel"),
        name="pointwise_glu",
    )(a, w, w, bias2, bias2)


def _conv_kernel(g_ref, halo_ref, w_ref, db_ref, lg_ref, lb_ref, o_ref, ext_ref, cv_ref,
                 *, ts, seq_tiles, channels):
    i = pl.program_id(0)
    n_lc = g_ref.shape[0]
    halo = halo_ref[...]
    ext_ref[:, 0:CONV_HALO, :] = jnp.where(i % seq_tiles == 0, jnp.zeros_like(halo), halo)
    ext_ref[:, CONV_HALO:CONV_HALO + ts, :] = g_ref[...]
    first_tap_row = CONV_HALO - (CONV_TAPS - 1)

    def conv_column(lc, row_sum):
        acc = jnp.zeros((ts, LANES), F32)
        for tap in range(CONV_TAPS):
            acc = acc + w_ref[lc, tap:tap + 1, :] * ext_ref[lc, pl.ds(first_tap_row + tap, ts), :]
        conv = acc + db_ref[lc]
        cv_ref[lc] = conv
        return row_sum + conv

    row_sum = lax.fori_loop(0, n_lc, conv_column, jnp.zeros((ts, LANES), F32))
    mu = jnp.sum(row_sum, axis=-1, keepdims=True) / channels

    def sq_column(lc, sq_sum):
        d = cv_ref[lc] - mu
        return sq_sum + d * d

    sq_sum = lax.fori_loop(0, n_lc, sq_column, jnp.zeros((ts, LANES), F32))
    rstd = lax.rsqrt(jnp.sum(sq_sum, axis=-1, keepdims=True) / channels + LN_EPS)
    for lc in range(n_lc):
        y = (cv_ref[lc] - mu) * rstd * lg_ref[lc] + lb_ref[lc]
        o_ref[:, lc * LANES:(lc + 1) * LANES] = (y * jax.nn.sigmoid(y)).astype(o_ref.dtype)


def _lane_columns(v):
    return v.reshape(-1, 1, LANES)


def _conv_ln_swish(g3, dw_w, dw_b, ln_g, ln_b, seq):
    n_lc, t, _ = g3.shape
    channels = n_lc * LANES
    ts = _tile(seq, 256)
    seq_tiles = seq // ts
    halo_per_tile = ts // CONV_HALO
    w3 = dw_w.reshape(CONV_TAPS, n_lc, LANES).transpose(1, 0, 2)
    vec = pl.BlockSpec((n_lc, 1, LANES), lambda i: (0, 0, 0))
    kern = functools.partial(_conv_kernel, ts=ts, seq_tiles=seq_tiles, channels=channels)
    return pl.pallas_call(
        kern,
        grid=(t // ts,),
        in_specs=[
            pl.BlockSpec((n_lc, ts, LANES), lambda i: (0, i, 0)),
            pl.BlockSpec((n_lc, CONV_HALO, LANES), lambda i: (0, jnp.maximum(i * halo_per_tile - 1, 0), 0)),
            pl.BlockSpec((n_lc, CONV_TAPS, LANES), lambda i: (0, 0, 0)),
            vec, vec, vec,
        ],
        out_specs=pl.BlockSpec((ts, channels), lambda i: (i, 0)),
        out_shape=jax.ShapeDtypeStruct((t, channels), BF16),
        scratch_shapes=[pltpu.VMEM((n_lc, CONV_HALO + ts, LANES), F32),
                        pltpu.VMEM((n_lc, ts, LANES), F32)],
        compiler_params=_params("parallel"),
        name="conv_ln_swish",
    )(g3, g3, w3, _lane_columns(dw_b), _lane_columns(ln_g), _lane_columns(ln_b))


def _split_bf16(x):
    hi = x.astype(BF16)
    lo = (x - hi.astype(F32)).astype(BF16)
    return hi, lo


def _rank_desc(vals, n):
    rows = lax.broadcasted_iota(jnp.int32, vals.shape, 0)
    rank = jnp.zeros(vals.shape, jnp.int32)
    for other in range(n):
        v = vals[other:other + 1, :]
        ahead = (v > vals) | ((v == vals) & (other < rows))
        rank = rank + ahead.astype(jnp.int32)
    return rank


def _router_kernel(h_ref, wt_ref, br_ref, e_ref, pos_ref, gate_ref, cnt_ref, carry_ref, *, tm):
    i = pl.program_id(0)

    @pl.when(i == 0)
    def _():
        carry_ref[...] = jnp.zeros_like(carry_ref)

    h_hi, h_lo = _split_bf16(h_ref[...])
    w_hi, w_lo = _split_bf16(wt_ref[...])
    nt = (((1,), (1,)), ((), ()))
    logits = (lax.dot_general(w_hi, h_hi, nt, preferred_element_type=F32)
              + lax.dot_general(w_hi, h_lo, nt, preferred_element_type=F32)
              + lax.dot_general(w_lo, h_hi, nt, preferred_element_type=F32))
    scores = jax.nn.sigmoid(logits)
    choice = scores + br_ref[...]

    group_scores = []
    for grp in range(N_EXPERT_GROUPS):
        cg = choice[grp * GROUP_SIZE:(grp + 1) * GROUP_SIZE, :]
        m1 = jnp.max(cg, axis=0, keepdims=True)
        is_max = cg == m1
        n_max = jnp.sum(is_max.astype(F32), axis=0, keepdims=True)
        below = jnp.max(jnp.where(is_max, -jnp.inf, cg), axis=0, keepdims=True)
        group_scores.append(m1 + jnp.where(n_max >= 2.0, m1, below))
    group_score = jnp.concatenate(group_scores, axis=0)
    group_ok = _rank_desc(group_score, N_EXPERT_GROUPS) < TOPK_GROUPS
    masked = jnp.concatenate(
        [jnp.where(group_ok[grp:grp + 1, :], choice[grp * GROUP_SIZE:(grp + 1) * GROUP_SIZE, :], -jnp.inf)
         for grp in range(N_EXPERT_GROUPS)], axis=0)
    rank = _rank_desc(masked, N_EXPERTS)
    chosen = rank < TOP_K

    picked = jnp.where(chosen, scores, 0.0)
    gate = picked / jnp.sum(picked, axis=0, keepdims=True) * ROUTED_SCALE

    onehot = chosen.astype(F32)
    earlier = (lax.broadcasted_iota(jnp.int32, (tm, tm), 0)
               < lax.broadcasted_iota(jnp.int32, (tm, tm), 1)).astype(BF16)
    before = jnp.dot(onehot.astype(BF16), earlier, preferred_element_type=F32) + carry_ref[...]
    carry_ref[...] = carry_ref[...] + jnp.sum(onehot, axis=1, keepdims=True)
    cnt_ref[...] = jnp.broadcast_to(carry_ref[...], cnt_ref.shape).astype(jnp.int32)

    expert_id = lax.broadcasted_iota(jnp.int32, (N_EXPERTS, tm), 0).astype(F32)
    slot_row = lax.broadcasted_iota(jnp.int32, (SUBLANES, tm), 0)
    e_out = jnp.zeros((SUBLANES, tm), F32)
    pos_out = jnp.zeros((SUBLANES, tm), F32)
    gate_out = jnp.zeros((SUBLANES, tm), F32)
    for slot in range(TOP_K):
        here = rank == slot
        e_out = jnp.where(slot_row == slot, jnp.sum(jnp.where(here, expert_id, 0.0), axis=0, keepdims=True), e_out)
        pos_out = jnp.where(slot_row == slot, jnp.sum(jnp.where(here, before, 0.0), axis=0, keepdims=True), pos_out)
        gate_out = jnp.where(slot_row == slot, jnp.sum(jnp.where(here, gate, 0.0), axis=0, keepdims=True), gate_out)
    e_ref[...] = e_out.astype(jnp.int32)
    pos_ref[...] = pos_out.astype(jnp.int32)
    gate_ref[...] = gate_out


def _route(h, w_router, b_router):
    t, d = h.shape
    tm = _tile(t, 256)
    slot_rows = pl.BlockSpec((SUBLANES, tm), lambda i: (0, i))
    return pl.pallas_call(
        functools.partial(_router_kernel, tm=tm),
        grid=(t // tm,),
        in_specs=[pl.BlockSpec((tm, d), lambda i: (i, 0)),
                  pl.BlockSpec((N_EXPERTS, d), lambda i: (0, 0)),
                  pl.BlockSpec((N_EXPERTS, 1), lambda i: (0, 0))],
        out_specs=[slot_rows, slot_rows, slot_rows,
                   pl.BlockSpec((N_EXPERTS, LANES), lambda i: (0, 0))],
        out_shape=[jax.ShapeDtypeStruct((SUBLANES, t), jnp.int32),
                   jax.ShapeDtypeStruct((SUBLANES, t), jnp.int32),
                   jax.ShapeDtypeStruct((SUBLANES, t), F32),
                   jax.ShapeDtypeStruct((N_EXPERTS, LANES), jnp.int32)],
        scratch_shapes=[pltpu.VMEM((N_EXPERTS, 1), F32)],
        compiler_params=_params("arbitrary"),
        name="router",
    )(h, w_router.T, b_router.reshape(N_EXPERTS, 1))


def _dispatch_kernel(pend_ref, padded_ref, nu_ref, dest_ref, h_ref, xs_ref, zero_ref, sem, zsem, *, tm):
    i = pl.program_id(0)
    n_blocks = xs_ref.shape[0] // EXPERT_BLOCK

    def zero_copy(start):
        start = pl.multiple_of(start, EXPERT_BLOCK)
        return pltpu.make_async_copy(zero_ref, xs_ref.at[pl.ds(start, EXPERT_BLOCK), :], zsem)

    def tail_zero_copy(e):
        return zero_copy(pend_ref[e] - EXPERT_BLOCK)

    def start_unused(b, carry):
        zero_copy(b * EXPERT_BLOCK).start()
        return carry

    def wait_unused(b, carry):
        zero_copy(b * EXPERT_BLOCK).wait()
        return carry

    @pl.when(i == 0)
    def _():
        zero_ref[...] = jnp.zeros_like(zero_ref)
        for e in range(N_EXPERTS):
            @pl.when(padded_ref[e] > 0)
            def _(e=e):
                tail_zero_copy(e).start()
        lax.fori_loop(nu_ref[0], n_blocks, start_unused, 0)
        for e in range(N_EXPERTS):
            @pl.when(padded_ref[e] > 0)
            def _(e=e):
                tail_zero_copy(e).wait()
        lax.fori_loop(nu_ref[0], n_blocks, wait_unused, 0)

    def row_copy(t, k):
        return pltpu.make_async_copy(h_ref.at[pl.ds(t, 1), :], xs_ref.at[pl.ds(dest_ref[k, t], 1), :], sem)

    def start_rows(t, carry):
        for k in range(TOP_K):
            row_copy(t, k).start()
        return carry

    def wait_rows(t, carry):
        for k in range(TOP_K):
            row_copy(t, k).wait()
        return carry

    lax.fori_loop(0, tm, start_rows, 0)
    lax.fori_loop(0, tm, wait_rows, 0)


def _dispatch(h, dest, pad_ends, padded, n_used, n_rows):
    t, d = h.shape
    tm = _tile(t, LANES)
    return pl.pallas_call(
        functools.partial(_dispatch_kernel, tm=tm),
        grid_spec=pltpu.PrefetchScalarGridSpec(
            num_scalar_prefetch=3,
            grid=(t // tm,),
            in_specs=[pl.BlockSpec((SUBLANES, tm), lambda i, pe, pd, nu: (0, i), memory_space=pltpu.SMEM),
                      pl.BlockSpec((tm, d), lambda i, pe, pd, nu: (i, 0))],
            out_specs=pl.BlockSpec(memory_space=pl.ANY),
            scratch_shapes=[pltpu.VMEM((EXPERT_BLOCK, d), F32),
                            pltpu.SemaphoreType.DMA(()), pltpu.SemaphoreType.DMA(())],
        ),
        out_shape=jax.ShapeDtypeStruct((n_rows, d), F32),
        compiler_params=_params("arbitrary"),
        name="dispatch",
    )(pad_ends, padded, n_used, dest, h)


def _expert_kernel(be_ref, nu_ref, x_ref, wgu_ref, wd_ref, y_ref):
    del be_ref
    used = pl.program_id(0) < nu_ref[0]

    @pl.when(used)
    def _():
        f = wd_ref.shape[0]
        gu = jnp.dot(x_ref[...].astype(BF16), wgu_ref[...], preferred_element_type=F32)
        gate, up = gu[:, :f], gu[:, f:]
        act = (gate * jax.nn.sigmoid(gate)) * up
        y_ref[...] = jnp.dot(act.astype(BF16), wd_ref[...], preferred_element_type=F32)

    @pl.when(jnp.logical_not(used))
    def _():
        y_ref[...] = jnp.zeros_like(y_ref)


def _grouped_swiglu(x, w_gu, w_down, block_expert, n_used, tm):
    rows, d = x.shape
    f2 = w_gu.shape[-1]
    f = w_down.shape[-2]

    def row_block(b, be, nu):
        return (jnp.minimum(b, nu[0] - 1), 0)

    return pl.pallas_call(
        _expert_kernel,
        grid_spec=pltpu.PrefetchScalarGridSpec(
            num_scalar_prefetch=2,
            grid=(rows // tm,),
            in_specs=[pl.BlockSpec((tm, d), row_block),
                      pl.BlockSpec((None, d, f2), lambda b, be, nu: (be[b], 0, 0)),
                      pl.BlockSpec((None, f, d), lambda b, be, nu: (be[b], 0, 0))],
            out_specs=pl.BlockSpec((tm, d), lambda b, be, nu: (b, 0)),
        ),
        out_shape=jax.ShapeDtypeStruct((rows, d), F32),
        compiler_params=_params("arbitrary"),
        name="grouped_swiglu",
    )(block_expert, n_used, x, w_gu, w_down)


def _combine_kernel(dcur_ref, dnext_ref, gate_ref, sh_ref, h_ref, lg_ref, lb_ref, y_ref,
                    o32_ref, o16_ref, buf_ref, sem, *, tm, alpha):
    i = pl.program_id(0)
    slot = i % 2

    def row_copy(dref, t, k, s):
        return pltpu.make_async_copy(y_ref.at[pl.ds(dref[k, t], 1), :],
                                     buf_ref.at[s, k, pl.ds(t, 1), :], sem.at[s])

    def start_tile(dref, s):
        def body(t, carry):
            for k in range(TOP_K):
                row_copy(dref, t, k, s).start()
            return carry
        lax.fori_loop(0, tm, body, 0)

    @pl.when(i == 0)
    def _():
        start_tile(dcur_ref, 0)

    @pl.when(i + 1 < pl.num_programs(0))
    def _():
        start_tile(dnext_ref, 1 - slot)

    def wait_body(t, carry):
        for k in range(TOP_K):
            row_copy(dcur_ref, t, k, slot).wait()
        return carry
    lax.fori_loop(0, tm, wait_body, 0)

    acc = alpha * h_ref[...] + sh_ref[...]
    for k in range(TOP_K):
        acc = acc + gate_ref[:, k:k + 1] * buf_ref[slot, k]
    y = _layer_norm_rows(acc, lg_ref[...], lb_ref[...])
    o32_ref[...] = y
    o16_ref[...] = y.astype(BF16)


def _combine_ln(dest, gates_t, shared, h, y, g, b, alpha):
    t, d = h.shape
    tm = _tile(t, LANES)
    n_tiles = t // tm
    row = pl.BlockSpec((tm, d), lambda i: (i, 0))
    vec = pl.BlockSpec((1, d), lambda i: (0, 0))
    return pl.pallas_call(
        functools.partial(_combine_kernel, tm=tm, alpha=alpha),
        grid=(n_tiles,),
        in_specs=[pl.BlockSpec((SUBLANES, tm), lambda i: (0, i), memory_space=pltpu.SMEM),
                  pl.BlockSpec((SUBLANES, tm), lambda i: (0, jnp.minimum(i + 1, n_tiles - 1)),
                               memory_space=pltpu.SMEM),
                  pl.BlockSpec((tm, SUBLANES), lambda i: (i, 0)),
                  row, row, vec, vec,
                  pl.BlockSpec(memory_space=pl.ANY)],
        out_specs=[row, row],
        out_shape=[jax.ShapeDtypeStruct((t, d), F32), jax.ShapeDtypeStruct((t, d), BF16)],
        scratch_shapes=[pltpu.VMEM((2, TOP_K, tm, d), F32), pltpu.SemaphoreType.DMA((2,))],
        compiler_params=_params("arbitrary"),
        name="combine_ln",
    )(dest, dest, gates_t, shared, h, g.reshape(1, d), b.reshape(1, d), y)


def _moe_ffn_ln(h32, h16, w_router, b_router, w_gu, w_down, w_sh_gu, w_sh_down, ln_g, ln_b, alpha):
    t, d = h32.shape
    blk = EXPERT_BLOCK
    n_blocks = -(-(t * TOP_K) // blk) + N_EXPERTS
    n_rows = n_blocks * blk

    e_k, pos_k, gate_k, counts = _route(h32, w_router, b_router)
    counts = counts[:, 0]
    padded = (counts + blk - 1) // blk * blk
    pad_ends = jnp.cumsum(padded).astype(jnp.int32)
    pad_starts = pad_ends - padded
    dest = pad_starts[e_k] + pos_k
    n_used = pad_ends[-1:] // blk
    block_start = jnp.arange(n_blocks, dtype=jnp.int32) * blk
    block_expert = jnp.minimum(jnp.sum(pad_ends[None, :] <= block_start[:, None], axis=1),
                               N_EXPERTS - 1).astype(jnp.int32)
    block_expert = jnp.where(jnp.arange(n_blocks) < n_used[0], block_expert,
                             block_expert[jnp.maximum(n_used[0] - 1, 0)])

    n_used = n_used.astype(jnp.int32)
    xs = _dispatch(h32, dest, pad_ends, padded.astype(jnp.int32), n_used, n_rows)
    y = _grouped_swiglu(xs, w_gu, w_down, block_expert, n_used, blk)

    tm_sh = _tile(t, 512)
    shared = _grouped_swiglu(h16, w_sh_gu[None], w_sh_down[None],
                             jnp.zeros((t // tm_sh,), jnp.int32),
                             jnp.full((1,), t // tm_sh, jnp.int32), tm_sh)
    return _combine_ln(dest, gate_k.T, shared, h32, y, ln_g, ln_b, alpha)


def kernel(x, w_in_ab, w_pool, pool_scale, sgu_ln_g, sgu_ln_b, sgu_w, sgu_b, w_out_ab, w_pw1, b_pw1, dw_w, dw_b, conv_ln_g, conv_ln_b, w_pw2, b_pw2, mix_ln_g, mix_ln_b, w_router, b_router, w_gu, w_down, w_shared_gu, w_shared_down, ffn_ln_g, ffn_ln_b):
    bsz, seq, d = x.shape
    depth = mix_ln_g.shape[0]
    alpha = (2 * depth) ** 0.25
    h32 = x.reshape(bsz * seq, d)
    h16 = h32.astype(BF16)
    w_gu16, w_down16 = w_gu.astype(BF16), w_down.astype(BF16)
    w_sh_gu16, w_sh_down16 = w_shared_gu.astype(BF16), w_shared_down.astype(BF16)
    for layer in range(depth):
        i = layer // 2
        if layer % 2 == 0:
            proj = _matmul(h16, w_in_ab[i].astype(BF16))
            ab = _pool_sgu_mix(proj, w_pool[i], pool_scale[i], sgu_ln_g[i], sgu_ln_b[i], sgu_w[i], sgu_b[i], seq)
            mix = _matmul(ab, w_out_ab[i].astype(BF16))
        else:
            g3 = _pointwise_glu(h16, w_pw1[i].astype(BF16), b_pw1[i])
            cb = _conv_ln_swish(g3, dw_w[i], dw_b[i], conv_ln_g[i], conv_ln_b[i], seq)
            mix = _matmul(cb, w_pw2[i].astype(BF16), bias=b_pw2[i])
        h32, h16 = _residual_ln(h32, mix, mix_ln_g[layer], mix_ln_b[layer], alpha)
        h32, h16 = _moe_ffn_ln(h32, h16, w_router[layer], b_router[layer], w_gu16[layer], w_down16[layer],
                               w_sh_gu16[layer], w_sh_down16[layer], ffn_ln_g[layer], ffn_ln_b[layer], alpha)
    return h32.reshape(bsz, seq, d)
```

```python
import functools

import jax
import jax.numpy as jnp
from jax import lax
from jax.experimental import pallas as pl
from jax.experimental.pallas import tpu as pltpu

F32 = jnp.float32
BF16 = jnp.bfloat16

POOL_WINDOWS = (2, 4, 8, 16)
POOL_HALO = 16
SGU_HEADS = 8
SGU_CHUNK = 128
CONV_TAPS = 31
CONV_HALO = 32
N_EXPERTS = 64
TOP_K = 6
N_EXPERT_GROUPS = 8
GROUP_SIZE = N_EXPERTS // N_EXPERT_GROUPS
TOPK_GROUPS = 4
ROUTED_SCALE = 2.5
EXPERT_BLOCK = 256
LN_EPS = 1e-5

LANES = 128
SUBLANES = 8
VMEM_LIMIT_BYTES = 56 * 1024 * 1024


def _params(*semantics):
    return pltpu.CompilerParams(dimension_semantics=semantics, vmem_limit_bytes=VMEM_LIMIT_BYTES)


def _tile(n, pref):
    t = min(n, pref)
    while n % t:
        t //= 2
    return t


def _layer_norm_rows(z, g, b):
    mu = jnp.mean(z, axis=-1, keepdims=True)
    zc = z - mu
    var = jnp.mean(zc * zc, axis=-1, keepdims=True)
    return zc * lax.rsqrt(var + LN_EPS) * g + b


def _mm_kernel(a_ref, b_ref, o_ref):
    o_ref[...] = jnp.dot(a_ref[...], b_ref[...], preferred_element_type=F32).astype(o_ref.dtype)


def _mm_bias_kernel(a_ref, b_ref, bias_ref, o_ref):
    acc = jnp.dot(a_ref[...], b_ref[...], preferred_element_type=F32)
    o_ref[...] = (acc + bias_ref[...]).astype(o_ref.dtype)


def _matmul(a, b, bias=None, out_dtype=F32):
    m, k = a.shape
    n = b.shape[1]
    tm, tn = _tile(m, 512), _tile(n, 1024)
    in_specs = [pl.BlockSpec((tm, k), lambda j, i: (i, 0)),
                pl.BlockSpec((k, tn), lambda j, i: (0, j))]
    args = [a, b]
    body = _mm_kernel
    if bias is not None:
        in_specs.append(pl.BlockSpec((1, tn), lambda j, i: (0, j)))
        args.append(bias.reshape(1, n))
        body = _mm_bias_kernel
    return pl.pallas_call(
        body,
        grid=(n // tn, m // tm),
        in_specs=in_specs,
        out_specs=pl.BlockSpec((tm, tn), lambda j, i: (i, j)),
        out_shape=jax.ShapeDtypeStruct((m, n), out_dtype),
        compiler_params=_params("parallel", "parallel"),
        name="dense_matmul",
    )(*args)


def _ln_kernel(h_ref, m_ref, g_ref, b_ref, o32_ref, o16_ref, *, alpha):
    y = _layer_norm_rows(alpha * h_ref[...] + m_ref[...], g_ref[...], b_ref[...])
    o32_ref[...] = y
    o16_ref[...] = y.astype(BF16)


def _residual_ln(h, mix, g, b, alpha):
    t, d = h.shape
    tm = _tile(t, 256)
    row = pl.BlockSpec((tm, d), lambda i: (i, 0))
    vec = pl.BlockSpec((1, d), lambda i: (0, 0))
    return pl.pallas_call(
        functools.partial(_ln_kernel, alpha=alpha),
        grid=(t // tm,),
        in_specs=[row, row, vec, vec],
        out_specs=[row, row],
        out_shape=[jax.ShapeDtypeStruct((t, d), F32), jax.ShapeDtypeStruct((t, d), BF16)],
        compiler_params=_params("parallel"),
        name="residual_ln",
    )(h, mix, g.reshape(1, d), b.reshape(1, d))


def _mix0_kernel(pa_ref, pv_ref, halo_ref, wp_ref, ps_ref, lg_ref, lb_ref, sw_ref, sb_ref,
                 o_ref, ext_ref, *, tm, seq_tiles, head_dim):
    i = pl.program_id(0)
    j = pl.program_id(1)
    n_pool = len(POOL_WINDOWS)
    seq_tile = i % seq_tiles

    for grp, window in enumerate(POOL_WINDOWS):
        @pl.when(j == grp)
        def _(window=window):
            a = pa_ref[...]
            halo = halo_ref[...]
            ext_ref[0:POOL_HALO, :] = jnp.where(seq_tile == 0, jnp.zeros_like(halo), halo)
            ext_ref[POOL_HALO:POOL_HALO + tm, :] = a
            wsum = a
            for back in range(1, window):
                wsum = wsum + ext_ref[pl.ds(POOL_HALO - back, tm), :]
            pos = seq_tile * tm + lax.broadcasted_iota(jnp.int32, (tm, 1), 0)
            count = jnp.minimum(pos + 1, window).astype(F32)
            pooled = wsum / count - a
            mapped = jnp.dot(pooled.astype(BF16), wp_ref[...], preferred_element_type=F32)
            o_ref[...] = (mapped * ps_ref[...]).astype(o_ref.dtype)

    @pl.when(j >= n_pool)
    def _():
        u = jax.nn.gelu(pa_ref[...])
        v = jax.nn.gelu(pv_ref[...])
        r = lax.broadcasted_iota(jnp.int32, (SGU_CHUNK, SGU_CHUNK), 0)
        c = lax.broadcasted_iota(jnp.int32, (SGU_CHUNK, SGU_CHUNK), 1)
        for hh in range(pa_ref.shape[1] // head_dim):
            cols = slice(hh * head_dim, (hh + 1) * head_dim)
            vn = _layer_norm_rows(v[:, cols], lg_ref[:, cols], lb_ref[:, cols]).astype(BF16)
            w_s = jnp.where(r >= c, sw_ref[hh], 0.0).astype(BF16)
            bias = sb_ref[hh]
            for ch in range(tm // SGU_CHUNK):
                rows = slice(ch * SGU_CHUNK, (ch + 1) * SGU_CHUNK)
                sv = jnp.dot(w_s, vn[rows], preferred_element_type=F32) + bias
                o_ref[rows, cols] = (u[rows, cols] * sv).astype(o_ref.dtype)


def _pool_sgu_mix(proj, w_pool, pool_scale, sgu_ln_g, sgu_ln_b, sgu_w, sgu_b, seq):
    t = proj.shape[0]
    n_pool = len(POOL_WINDOWS)
    gd = w_pool.shape[-1]
    pool_w = n_pool * gd
    sgu_w_total = sgu_ln_g.shape[-1]
    head_dim = sgu_w_total // SGU_HEADS
    hps = gd // head_dim
    n_sgu = sgu_w_total // gd
    tm = _tile(seq, 512)
    seq_tiles = seq // tm
    v0 = (pool_w + sgu_w_total) // gd
    halo_per_tile = tm // POOL_HALO

    kern = functools.partial(_mix0_kernel, tm=tm, seq_tiles=seq_tiles, head_dim=head_dim)
    return pl.pallas_call(
        kern,
        grid=(t // tm, n_pool + n_sgu),
        in_specs=[
            pl.BlockSpec((tm, gd), lambda i, j: (i, j)),
            pl.BlockSpec((tm, gd), lambda i, j: (i, jnp.maximum(j - n_pool, 0) + v0)),
            pl.BlockSpec((POOL_HALO, gd),
                         lambda i, j: (jnp.maximum(i * halo_per_tile - 1, 0), jnp.minimum(j, n_pool - 1))),
            pl.BlockSpec((None, gd, gd), lambda i, j: (jnp.minimum(j, n_pool - 1), 0, 0)),
            pl.BlockSpec((1, gd), lambda i, j: (0, jnp.minimum(j, n_pool - 1))),
            pl.BlockSpec((1, gd), lambda i, j: (0, jnp.maximum(j - n_pool, 0))),
            pl.BlockSpec((1, gd), lambda i, j: (0, jnp.maximum(j - n_pool, 0))),
            pl.BlockSpec((hps, SGU_CHUNK, SGU_CHUNK), lambda i, j: (jnp.maximum(j - n_pool, 0), 0, 0)),
            pl.BlockSpec((hps, SGU_CHUNK, 1), lambda i, j: (jnp.maximum(j - n_pool, 0), 0, 0)),
        ],
        out_specs=pl.BlockSpec((tm, gd), lambda i, j: (i, j)),
        out_shape=jax.ShapeDtypeStruct((t, pool_w + sgu_w_total), BF16),
        scratch_shapes=[pltpu.VMEM((POOL_HALO + tm, gd), F32)],
        compiler_params=_params("parallel", "arbitrary"),
        name="pool_sgu_mix",
    )(proj, proj, proj, w_pool.astype(BF16), pool_scale.reshape(1, pool_w),
      sgu_ln_g.reshape(1, -1), sgu_ln_b.reshape(1, -1), sgu_w, sgu_b[:, :, None])


def _glu_kernel(a_ref, bv_ref, bg_ref, biasv_ref, biasg_ref, o_ref):
    a = a_ref[...]
    val = jnp.dot(a, bv_ref[...], preferred_element_type=F32) + biasv_ref[...]
    gate = jnp.dot(a, bg_ref[...], preferred_element_type=F32) + biasg_ref[...]
    res = val * jax.nn.sigmoid(gate)
    for lc in range(o_ref.shape[0]):
        o_ref[lc] = res[:, lc * LANES:(lc + 1) * LANES]


def _pointwise_glu(a, w, bias):
    m, k = a.shape
    c = w.shape[1] // 2
    tm, tn = _tile(m, 512), _tile(c, 512)
    gate0 = c // tn
    bias2 = bias.reshape(1, 2 * c)
    return pl.pallas_call(
        _glu_kernel,
        grid=(c // tn, m // tm),
        in_specs=[
            pl.BlockSpec((tm, k), lambda j, i: (i, 0)),
            pl.BlockSpec((k, tn), lambda j, i: (0, j)),
            pl.BlockSpec((k, tn), lambda j, i: (0, j + gate0)),
            pl.BlockSpec((1, tn), lambda j, i: (0, j)),
            pl.BlockSpec((1, tn), lambda j, i: (0, j + gate0)),
        ],
        out_specs=pl.BlockSpec((tn // LANES, tm, LANES), lambda j, i: (j, i, 0)),
        out_shape=jax.ShapeDtypeStruct((c // LANES, m, LANES), F32),
        compiler_params=_params("parallel", "parallel"),
        name="pointwise_glu",
    )(a, w, w, bias2, bias2)


def _conv_kernel(g_ref, halo_ref, w_ref, db_ref, lg_ref, lb_ref, o_ref, ext_ref, cv_ref,
                 *, ts, seq_tiles, channels):
    i = pl.program_id(0)
    n_lc = g_ref.shape[0]
    halo = halo_ref[...]
    ext_ref[:, 0:CONV_HALO, :] = jnp.where(i % seq_tiles == 0, jnp.zeros_like(halo), halo)
    ext_ref[:, CONV_HALO:CONV_HALO + ts, :] = g_ref[...]
    first_tap_row = CONV_HALO - (CONV_TAPS - 1)

    def conv_column(lc, row_sum):
        acc = jnp.zeros((ts, LANES), F32)
        for tap in range(CONV_TAPS):
            acc = acc + w_ref[lc, tap:tap + 1, :] * ext_ref[lc, pl.ds(first_tap_row + tap, ts), :]
        conv = acc + db_ref[lc]
        cv_ref[lc] = conv
        return row_sum + conv

    row_sum = lax.fori_loop(0, n_lc, conv_column, jnp.zeros((ts, LANES), F32))
    mu = jnp.sum(row_sum, axis=-1, keepdims=True) / channels

    def sq_column(lc, sq_sum):
        d = cv_ref[lc] - mu
        return sq_sum + d * d

    sq_sum = lax.fori_loop(0, n_lc, sq_column, jnp.zeros((ts, LANES), F32))
    rstd = lax.rsqrt(jnp.sum(sq_sum, axis=-1, keepdims=True) / channels + LN_EPS)
    for lc in range(n_lc):
        y = (cv_ref[lc] - mu) * rstd * lg_ref[lc] + lb_ref[lc]
        o_ref[:, lc * LANES:(lc + 1) * LANES] = (y * jax.nn.sigmoid(y)).astype(o_ref.dtype)


def _lane_columns(v):
    return v.reshape(-1, 1, LANES)


def _conv_ln_swish(g3, dw_w, dw_b, ln_g, ln_b, seq):
    n_lc, t, _ = g3.shape
    channels = n_lc * LANES
    ts = _tile(seq, 256)
    seq_tiles = seq // ts
    halo_per_tile = ts // CONV_HALO
    w3 = dw_w.reshape(CONV_TAPS, n_lc, LANES).transpose(1, 0, 2)
    vec = pl.BlockSpec((n_lc, 1, LANES), lambda i: (0, 0, 0))
    kern = functools.partial(_conv_kernel, ts=ts, seq_tiles=seq_tiles, channels=channels)
    return pl.pallas_call(
        kern,
        grid=(t // ts,),
        in_specs=[
            pl.BlockSpec((n_lc, ts, LANES), lambda i: (0, i, 0)),
            pl.BlockSpec((n_lc, CONV_HALO, LANES), lambda i: (0, jnp.maximum(i * halo_per_tile - 1, 0), 0)),
            pl.BlockSpec((n_lc, CONV_TAPS, LANES), lambda i: (0, 0, 0)),
            vec, vec, vec,
        ],
        out_specs=pl.BlockSpec((ts, channels), lambda i: (i, 0)),
        out_shape=jax.ShapeDtypeStruct((t, channels), BF16),
        scratch_shapes=[pltpu.VMEM((n_lc, CONV_HALO + ts, LANES), F32),
                        pltpu.VMEM((n_lc, ts, LANES), F32)],
        compiler_params=_params("parallel"),
        name="conv_ln_swish",
    )(g3, g3, w3, _lane_columns(dw_b), _lane_columns(ln_g), _lane_columns(ln_b))


def _split_bf16(x):
    hi = x.astype(BF16)
    lo = (x - hi.astype(F32)).astype(BF16)
    return hi, lo


def _rank_desc(vals, n):
    rows = lax.broadcasted_iota(jnp.int32, vals.shape, 0)
    rank = jnp.zeros(vals.shape, jnp.int32)
    for other in range(n):
        v = vals[other:other + 1, :]
        ahead = (v > vals) | ((v == vals) & (other < rows))
        rank = rank + ahead.astype(jnp.int32)
    return rank


def _router_kernel(h_ref, wt_ref, br_ref, e_ref, pos_ref, gate_ref, cnt_ref, carry_ref, *, tm):
    i = pl.program_id(0)

    @pl.when(i == 0)
    def _():
        carry_ref[...] = jnp.zeros_like(carry_ref)

    h_hi, h_lo = _split_bf16(h_ref[...])
    w_hi, w_lo = _split_bf16(wt_ref[...])
    nt = (((1,), (1,)), ((), ()))
    logits = (lax.dot_general(w_hi, h_hi, nt, preferred_element_type=F32)
              + lax.dot_general(w_hi, h_lo, nt, preferred_element_type=F32)
              + lax.dot_general(w_lo, h_hi, nt, preferred_element_type=F32))
    scores = jax.nn.sigmoid(logits)
    choice = scores + br_ref[...]

    group_scores = []
    for grp in range(N_EXPERT_GROUPS):
        cg = choice[grp * GROUP_SIZE:(grp + 1) * GROUP_SIZE, :]
        m1 = jnp.max(cg, axis=0, keepdims=True)
        is_max = cg == m1
        n_max = jnp.sum(is_max.astype(F32), axis=0, keepdims=True)
        below = jnp.max(jnp.where(is_max, -jnp.inf, cg), axis=0, keepdims=True)
        group_scores.append(m1 + jnp.where(n_max >= 2.0, m1, below))
    group_score = jnp.concatenate(group_scores, axis=0)
    group_ok = _rank_desc(group_score, N_EXPERT_GROUPS) < TOPK_GROUPS
    masked = jnp.concatenate(
        [jnp.where(group_ok[grp:grp + 1, :], choice[grp * GROUP_SIZE:(grp + 1) * GROUP_SIZE, :], -jnp.inf)
         for grp in range(N_EXPERT_GROUPS)], axis=0)
    rank = _rank_desc(masked, N_EXPERTS)
    chosen = rank < TOP_K

    picked = jnp.where(chosen, scores, 0.0)
    gate = picked / jnp.sum(picked, axis=0, keepdims=True) * ROUTED_SCALE

    onehot = chosen.astype(F32)
    earlier = (lax.broadcasted_iota(jnp.int32, (tm, tm), 0)
               < lax.broadcasted_iota(jnp.int32, (tm, tm), 1)).astype(BF16)
    before = jnp.dot(onehot.astype(BF16), earlier, preferred_element_type=F32) + carry_ref[...]
    carry_ref[...] = carry_ref[...] + jnp.sum(onehot, axis=1, keepdims=True)
    cnt_ref[...] = jnp.broadcast_to(carry_ref[...], cnt_ref.shape).astype(jnp.int32)

    expert_id = lax.broadcasted_iota(jnp.int32, (N_EXPERTS, tm), 0).astype(F32)
    slot_row = lax.broadcasted_iota(jnp.int32, (SUBLANES, tm), 0)
    e_out = jnp.zeros((SUBLANES, tm), F32)
    pos_out = jnp.zeros((SUBLANES, tm), F32)
    gate_out = jnp.zeros((SUBLANES, tm), F32)
    for slot in range(TOP_K):
        here = rank == slot
        e_out = jnp.where(slot_row == slot, jnp.sum(jnp.where(here, expert_id, 0.0), axis=0, keepdims=True), e_out)
        pos_out = jnp.where(slot_row == slot, jnp.sum(jnp.where(here, before, 0.0), axis=0, keepdims=True), pos_out)
        gate_out = jnp.where(slot_row == slot, jnp.sum(jnp.where(here, gate, 0.0), axis=0, keepdims=True), gate_out)
    e_ref[...] = e_out.astype(jnp.int32)
    pos_ref[...] = pos_out.astype(jnp.int32)
    gate_ref[...] = gate_out


def _route(h, w_router, b_router):
    t, d = h.shape
    tm = _tile(t, 256)
    slot_rows = pl.BlockSpec((SUBLANES, tm), lambda i: (0, i))
    return pl.pallas_call(
        functools.partial(_router_kernel, tm=tm),
        grid=(t // tm,),
        in_specs=[pl.BlockSpec((tm, d), lambda i: (i, 0)),
                  pl.BlockSpec((N_EXPERTS, d), lambda i: (0, 0)),
                  pl.BlockSpec((N_EXPERTS, 1), lambda i: (0, 0))],
        out_specs=[slot_rows, slot_rows, slot_rows,
                   pl.BlockSpec((N_EXPERTS, LANES), lambda i: (0, 0))],
        out_shape=[jax.ShapeDtypeStruct((SUBLANES, t), jnp.int32),
                   jax.ShapeDtypeStruct((SUBLANES, t), jnp.int32),
                   jax.ShapeDtypeStruct((SUBLANES, t), F32),
                   jax.ShapeDtypeStruct((N_EXPERTS, LANES), jnp.int32)],
        scratch_shapes=[pltpu.VMEM((N_EXPERTS, 1), F32)],
        compiler_params=_params("arbitrary"),
        name="router",
    )(h, w_router.T, b_router.reshape(N_EXPERTS, 1))


HIGH_HALF = 0xFFFF0000


def _pack_bf16_pairs(x16):
    half = x16.shape[1] // 2
    lo = lax.bitcast_convert_type(x16[:, :half].astype(F32), jnp.uint32) >> 16
    hi = lax.bitcast_convert_type(x16[:, half:].astype(F32), jnp.uint32) & jnp.uint32(HIGH_HALF)
    return lo | hi


def _unpack_bf16_pairs(words):
    lo = lax.bitcast_convert_type(words << 16, F32)
    hi = lax.bitcast_convert_type(words & jnp.uint32(HIGH_HALF), F32)
    return jnp.concatenate([lo, hi], axis=1).astype(BF16)


def _dispatch_kernel(pend_ref, padded_ref, nu_ref, dest_ref, h16_ref, xs_ref, h_ref, zero_ref, sem, zsem, *, tm):
    i = pl.program_id(0)
    n_blocks = xs_ref.shape[0] // EXPERT_BLOCK
    h_ref[...] = _pack_bf16_pairs(h16_ref[...])

    def zero_copy(start):
        start = pl.multiple_of(start, EXPERT_BLOCK)
        return pltpu.make_async_copy(zero_ref, xs_ref.at[pl.ds(start, EXPERT_BLOCK), :], zsem)

    def tail_zero_copy(e):
        return zero_copy(pend_ref[e] - EXPERT_BLOCK)

    def start_unused(b, carry):
        zero_copy(b * EXPERT_BLOCK).start()
        return carry

    def wait_unused(b, carry):
        zero_copy(b * EXPERT_BLOCK).wait()
        return carry

    @pl.when(i == 0)
    def _():
        zero_ref[...] = jnp.zeros_like(zero_ref)
        for e in range(N_EXPERTS):
            @pl.when(padded_ref[e] > 0)
            def _(e=e):
                tail_zero_copy(e).start()
        lax.fori_loop(nu_ref[0], n_blocks, start_unused, 0)
        for e in range(N_EXPERTS):
            @pl.when(padded_ref[e] > 0)
            def _(e=e):
                tail_zero_copy(e).wait()
        lax.fori_loop(nu_ref[0], n_blocks, wait_unused, 0)

    def row_copy(t, k):
        return pltpu.make_async_copy(h_ref.at[pl.ds(t, 1), :], xs_ref.at[pl.ds(dest_ref[k, t], 1), :], sem)

    def start_rows(t, carry):
        for k in range(TOP_K):
            row_copy(t, k).start()
        return carry

    def wait_rows(t, carry):
        for k in range(TOP_K):
            row_copy(t, k).wait()
        return carry

    lax.fori_loop(0, tm, start_rows, 0)
    lax.fori_loop(0, tm, wait_rows, 0)


def _dispatch(h16, dest, pad_ends, padded, n_used, n_rows):
    t, d = h16.shape
    tm = _tile(t, LANES)
    words = d // 2
    return pl.pallas_call(
        functools.partial(_dispatch_kernel, tm=tm),
        grid_spec=pltpu.PrefetchScalarGridSpec(
            num_scalar_prefetch=3,
            grid=(t // tm,),
            in_specs=[pl.BlockSpec((SUBLANES, tm), lambda i, pe, pd, nu: (0, i), memory_space=pltpu.SMEM),
                      pl.BlockSpec((tm, d), lambda i, pe, pd, nu: (i, 0))],
            out_specs=pl.BlockSpec(memory_space=pl.ANY),
            scratch_shapes=[pltpu.VMEM((tm, words), jnp.uint32),
                            pltpu.VMEM((EXPERT_BLOCK, words), jnp.uint32),
                            pltpu.SemaphoreType.DMA(()), pltpu.SemaphoreType.DMA(())],
        ),
        out_shape=jax.ShapeDtypeStruct((n_rows, words), jnp.uint32),
        compiler_params=_params("arbitrary"),
        name="dispatch",
    )(pad_ends, padded, n_used, dest, h16)


CAST_CHUNK_ELEMS = 256 * 1024


def _cast_rows_to_bf16(src_ref, dst_ref):
    chunk = _tile(src_ref.shape[0], CAST_CHUNK_ELEMS // src_ref.shape[1])

    def body(i, carry):
        r = pl.multiple_of(i * chunk, chunk)
        dst_ref[pl.ds(r, chunk), :] = src_ref[pl.ds(r, chunk), :].astype(BF16)
        return carry

    lax.fori_loop(0, src_ref.shape[0] // chunk, body, 0)


def _expert_kernel(be_ref, first_ref, next_ref, nu_ref, x_ref, wgu_hbm, wd_hbm, y_ref,
                   stage_gu_ref, stage_d_ref, wgu_ref, wd_ref, sem, *, packed):
    b = pl.program_id(0)
    used = b < nu_ref[0]

    def weight_copies(e):
        return (pltpu.make_async_copy(wgu_hbm.at[e], stage_gu_ref, sem.at[0]),
                pltpu.make_async_copy(wd_hbm.at[e], stage_d_ref, sem.at[1]))

    @pl.when(jnp.logical_and(used, first_ref[b] == 1))
    def _():
        @pl.when(b == 0)
        def _():
            for cp in weight_copies(be_ref[0]):
                cp.start()
        for cp in weight_copies(be_ref[b]):
            cp.wait()
        _cast_rows_to_bf16(stage_gu_ref, wgu_ref)
        _cast_rows_to_bf16(stage_d_ref, wd_ref)

        @pl.when(next_ref[b] >= 0)
        def _():
            for cp in weight_copies(next_ref[b]):
                cp.start()

    @pl.when(used)
    def _():
        f = wd_ref.shape[0]
        x = _unpack_bf16_pairs(x_ref[...]) if packed else x_ref[...]
        gu = jnp.dot(x, wgu_ref[...], preferred_element_type=F32)
        gate, up = gu[:, :f], gu[:, f:]
        act = (gate * jax.nn.sigmoid(gate)) * up
        y_ref[...] = jnp.dot(act.astype(BF16), wd_ref[...], preferred_element_type=F32)

    @pl.when(jnp.logical_not(used))
    def _():
        y_ref[...] = jnp.zeros_like(y_ref)


def _grouped_swiglu(x, w_gu, w_down, block_expert, block_first, block_next, n_used, tm):
    rows = x.shape[0]
    _, d, f2 = w_gu.shape
    f = w_down.shape[1]
    packed = x.dtype == jnp.uint32

    def row_block(b, be, bf, bn, nu):
        return (jnp.minimum(b, nu[0] - 1), 0)

    return pl.pallas_call(
        functools.partial(_expert_kernel, packed=packed),
        grid_spec=pltpu.PrefetchScalarGridSpec(
            num_scalar_prefetch=4,
            grid=(rows // tm,),
            in_specs=[pl.BlockSpec((tm, x.shape[1]), row_block),
                      pl.BlockSpec(memory_space=pl.ANY),
                      pl.BlockSpec(memory_space=pl.ANY)],
            out_specs=pl.BlockSpec((tm, d), lambda b, be, bf, bn, nu: (b, 0)),
            scratch_shapes=[pltpu.VMEM((d, f2), F32), pltpu.VMEM((f, d), F32),
                            pltpu.VMEM((d, f2), BF16), pltpu.VMEM((f, d), BF16),
                            pltpu.SemaphoreType.DMA((2,))],
        ),
        out_shape=jax.ShapeDtypeStruct((rows, d), F32),
        compiler_params=_params("arbitrary"),
        name="grouped_swiglu",
    )(block_expert, block_first, block_next, n_used, x, w_gu, w_down)


def _combine_kernel(dcur_ref, dnext_ref, gate_ref, sh_ref, h_ref, lg_ref, lb_ref, y_ref,
                    o32_ref, o16_ref, buf_ref, sem, *, tm, alpha):
    i = pl.program_id(0)
    slot = i % 2

    def row_copy(dref, t, k, s):
        return pltpu.make_async_copy(y_ref.at[pl.ds(dref[k, t], 1), :],
                                     buf_ref.at[s, k, pl.ds(t, 1), :], sem.at[s])

    def start_tile(dref, s):
        def body(t, carry):
            for k in range(TOP_K):
                row_copy(dref, t, k, s).start()
            return carry
        lax.fori_loop(0, tm, body, 0)

    @pl.when(i == 0)
    def _():
        start_tile(dcur_ref, 0)

    @pl.when(i + 1 < pl.num_programs(0))
    def _():
        start_tile(dnext_ref, 1 - slot)

    def wait_body(t, carry):
        for k in range(TOP_K):
            row_copy(dcur_ref, t, k, slot).wait()
        return carry
    lax.fori_loop(0, tm, wait_body, 0)

    acc = alpha * h_ref[...] + sh_ref[...]
    for k in range(TOP_K):
        acc = acc + gate_ref[:, k:k + 1] * buf_ref[slot, k]
    y = _layer_norm_rows(acc, lg_ref[...], lb_ref[...])
    o32_ref[...] = y
    o16_ref[...] = y.astype(BF16)


def _combine_ln(dest, gates_t, shared, h, y, g, b, alpha):
    t, d = h.shape
    tm = _tile(t, LANES)
    n_tiles = t // tm
    row = pl.BlockSpec((tm, d), lambda i: (i, 0))
    vec = pl.BlockSpec((1, d), lambda i: (0, 0))
    return pl.pallas_call(
        functools.partial(_combine_kernel, tm=tm, alpha=alpha),
        grid=(n_tiles,),
        in_specs=[pl.BlockSpec((SUBLANES, tm), lambda i: (0, i), memory_space=pltpu.SMEM),
                  pl.BlockSpec((SUBLANES, tm), lambda i: (0, jnp.minimum(i + 1, n_tiles - 1)),
                               memory_space=pltpu.SMEM),
                  pl.BlockSpec((tm, SUBLANES), lambda i: (i, 0)),
                  row, row, vec, vec,
                  pl.BlockSpec(memory_space=pl.ANY)],
        out_specs=[row, row],
        out_shape=[jax.ShapeDtypeStruct((t, d), F32), jax.ShapeDtypeStruct((t, d), BF16)],
        scratch_shapes=[pltpu.VMEM((2, TOP_K, tm, d), F32), pltpu.SemaphoreType.DMA((2,))],
        compiler_params=_params("arbitrary"),
        name="combine_ln",
    )(dest, dest, gates_t, shared, h, g.reshape(1, d), b.reshape(1, d), y)


def _table_lookup(table, idx):
    n = table.shape[0]
    hit = idx[None] == jnp.arange(n, dtype=idx.dtype).reshape((n,) + (1,) * idx.ndim)
    return jnp.sum(jnp.where(hit, table.reshape((n,) + (1,) * idx.ndim), 0), axis=0)


def _moe_ffn_ln(h32, h16, layer, w_router, b_router, w_gu, w_down, w_sh_gu, w_sh_down, ln_g, ln_b, alpha):
    t, d = h32.shape
    blk = EXPERT_BLOCK
    n_blocks = -(-(t * TOP_K) // blk) + N_EXPERTS
    n_rows = n_blocks * blk
    i32 = jnp.int32

    e_k, pos_k, gate_k, counts = _route(h32, w_router[layer], b_router[layer])
    counts = counts[:, 0]
    padded = (counts + blk - 1) // blk * blk
    pad_ends = jnp.cumsum(padded).astype(i32)
    pad_starts = pad_ends - padded
    dest = _table_lookup(pad_starts, e_k) + pos_k
    n_used = (pad_ends[-1:] // blk).astype(i32)
    block_id = jnp.arange(n_blocks, dtype=i32)
    block_expert = jnp.minimum(jnp.sum(pad_ends[None, :] <= (block_id * blk)[:, None], axis=1),
                               N_EXPERTS - 1).astype(i32)
    last_expert = jnp.sum(jnp.where(block_id == n_used[0] - 1, block_expert, 0))
    block_expert = jnp.where(block_id < n_used[0], block_expert, last_expert)
    block_first = jnp.logical_or(block_id == 0, block_expert != jnp.roll(block_expert, 1)).astype(i32)
    expert_id = jnp.arange(N_EXPERTS, dtype=i32)
    owner = jnp.where(padded > 0, expert_id, N_EXPERTS)
    at_or_after = lax.cummin(owner[::-1])[::-1]
    after = jnp.concatenate([at_or_after[1:], jnp.full((1,), N_EXPERTS, i32)])
    first_row = layer * N_EXPERTS
    next_expert = jnp.where(after < N_EXPERTS, after + first_row, -1).astype(i32)
    block_next = _table_lookup(next_expert, block_expert)

    xs = _dispatch(h16, dest, pad_ends, padded.astype(i32), n_used, n_rows)
    y = _grouped_swiglu(xs, w_gu.reshape((-1,) + w_gu.shape[2:]), w_down.reshape((-1,) + w_down.shape[2:]),
                        block_expert + first_row, block_first, block_next, n_used, blk)

    n_sh = t // blk
    shared = _grouped_swiglu(h16, w_sh_gu, w_sh_down,
                             jnp.full((n_sh,), layer, i32), (jnp.arange(n_sh) == 0).astype(i32),
                             jnp.full((n_sh,), -1, i32), jnp.full((1,), n_sh, i32), blk)
    return _combine_ln(dest, gate_k.T, shared, h32, y, ln_g, ln_b, alpha)


def kernel(x, w_in_ab, w_pool, pool_scale, sgu_ln_g, sgu_ln_b, sgu_w, sgu_b, w_out_ab, w_pw1, b_pw1, dw_w, dw_b, conv_ln_g, conv_ln_b, w_pw2, b_pw2, mix_ln_g, mix_ln_b, w_router, b_router, w_gu, w_down, w_shared_gu, w_shared_down, ffn_ln_g, ffn_ln_b):
    bsz, seq, d = x.shape
    depth = mix_ln_g.shape[0]
    alpha = (2 * depth) ** 0.25
    h32 = x.reshape(bsz * seq, d)
    h16 = h32.astype(BF16)
    for layer in range(depth):
        i = layer // 2
        if layer % 2 == 0:
            proj = _matmul(h16, w_in_ab[i].astype(BF16))
            ab = _pool_sgu_mix(proj, w_pool[i], pool_scale[i], sgu_ln_g[i], sgu_ln_b[i], sgu_w[i], sgu_b[i], seq)
            mix = _matmul(ab, w_out_ab[i].astype(BF16))
        else:
            g3 = _pointwise_glu(h16, w_pw1[i].astype(BF16), b_pw1[i])
            cb = _conv_ln_swish(g3, dw_w[i], dw_b[i], conv_ln_g[i], conv_ln_b[i], seq)
            mix = _matmul(cb, w_pw2[i].astype(BF16), bias=b_pw2[i])
        h32, h16 = _residual_ln(h32, mix, mix_ln_g[layer], mix_ln_b[layer], alpha)
        h32, h16 = _moe_ffn_ln(h32, h16, layer, w_router, b_router, w_gu, w_down, w_shared_gu, w_shared_down,
                               ffn_ln_g[layer], ffn_ln_b[layer], alpha)
    return h32.reshape(bsz, seq, d)
```

```python
import functools

import jax
import jax.numpy as jnp
from jax import lax
from jax.experimental import pallas as pl
from jax.experimental.pallas import tpu as pltpu

F32 = jnp.float32
BF16 = jnp.bfloat16

POOL_WINDOWS = (2, 4, 8, 16)
POOL_HALO = 16
SGU_HEADS = 8
SGU_CHUNK = 128
CONV_TAPS = 31
CONV_HALO = 32
N_EXPERTS = 64
TOP_K = 6
N_EXPERT_GROUPS = 8
GROUP_SIZE = N_EXPERTS // N_EXPERT_GROUPS
TOPK_GROUPS = 4
ROUTED_SCALE = 2.5
EXPERT_BLOCK = 256
LN_EPS = 1e-5

LANES = 128
SUBLANES = 8
VMEM_LIMIT_BYTES = 56 * 1024 * 1024


def _params(*semantics):
    return pltpu.CompilerParams(dimension_semantics=semantics, vmem_limit_bytes=VMEM_LIMIT_BYTES)


def _tile(n, pref):
    t = min(n, pref)
    while n % t:
        t //= 2
    return t


def _layer_norm_rows(z, g, b):
    mu = jnp.mean(z, axis=-1, keepdims=True)
    zc = z - mu
    var = jnp.mean(zc * zc, axis=-1, keepdims=True)
    return zc * lax.rsqrt(var + LN_EPS) * g + b


def _mm_kernel(a_ref, b_ref, o_ref):
    o_ref[...] = jnp.dot(a_ref[...], b_ref[...], preferred_element_type=F32).astype(o_ref.dtype)


def _mm_bias_kernel(a_ref, b_ref, bias_ref, o_ref):
    acc = jnp.dot(a_ref[...], b_ref[...], preferred_element_type=F32)
    o_ref[...] = (acc + bias_ref[...]).astype(o_ref.dtype)


def _matmul(a, b, bias=None, out_dtype=F32):
    m, k = a.shape
    n = b.shape[1]
    tm, tn = _tile(m, 512), _tile(n, 1024)
    in_specs = [pl.BlockSpec((tm, k), lambda j, i: (i, 0)),
                pl.BlockSpec((k, tn), lambda j, i: (0, j))]
    args = [a, b]
    body = _mm_kernel
    if bias is not None:
        in_specs.append(pl.BlockSpec((1, tn), lambda j, i: (0, j)))
        args.append(bias.reshape(1, n))
        body = _mm_bias_kernel
    return pl.pallas_call(
        body,
        grid=(n // tn, m // tm),
        in_specs=in_specs,
        out_specs=pl.BlockSpec((tm, tn), lambda j, i: (i, j)),
        out_shape=jax.ShapeDtypeStruct((m, n), out_dtype),
        compiler_params=_params("parallel", "parallel"),
        name="dense_matmul",
    )(*args)


def _mix0_kernel(pa_ref, pv_ref, halo_ref, wp_ref, ps_ref, lg_ref, lb_ref, sw_ref, sb_ref,
                 o_ref, ext_ref, *, tm, seq_tiles, head_dim):
    i = pl.program_id(0)
    j = pl.program_id(1)
    n_pool = len(POOL_WINDOWS)
    seq_tile = i % seq_tiles

    for grp, window in enumerate(POOL_WINDOWS):
        @pl.when(j == grp)
        def _(window=window):
            a = pa_ref[...]
            halo = halo_ref[...]
            ext_ref[0:POOL_HALO, :] = jnp.where(seq_tile == 0, jnp.zeros_like(halo), halo)
            ext_ref[POOL_HALO:POOL_HALO + tm, :] = a
            wsum = a
            for back in range(1, window):
                wsum = wsum + ext_ref[pl.ds(POOL_HALO - back, tm), :]
            pos = seq_tile * tm + lax.broadcasted_iota(jnp.int32, (tm, 1), 0)
            count = jnp.minimum(pos + 1, window).astype(F32)
            pooled = wsum / count - a
            mapped = jnp.dot(pooled.astype(BF16), wp_ref[...], preferred_element_type=F32)
            o_ref[...] = (mapped * ps_ref[...]).astype(o_ref.dtype)

    @pl.when(j >= n_pool)
    def _():
        u = jax.nn.gelu(pa_ref[...])
        v = jax.nn.gelu(pv_ref[...])
        r = lax.broadcasted_iota(jnp.int32, (SGU_CHUNK, SGU_CHUNK), 0)
        c = lax.broadcasted_iota(jnp.int32, (SGU_CHUNK, SGU_CHUNK), 1)
        for hh in range(pa_ref.shape[1] // head_dim):
            cols = slice(hh * head_dim, (hh + 1) * head_dim)
            vn = _layer_norm_rows(v[:, cols], lg_ref[:, cols], lb_ref[:, cols]).astype(BF16)
            w_s = jnp.where(r >= c, sw_ref[hh], 0.0).astype(BF16)
            bias = sb_ref[hh]
            for ch in range(tm // SGU_CHUNK):
                rows = slice(ch * SGU_CHUNK, (ch + 1) * SGU_CHUNK)
                sv = jnp.dot(w_s, vn[rows], preferred_element_type=F32) + bias
                o_ref[rows, cols] = (u[rows, cols] * sv).astype(o_ref.dtype)


def _pool_sgu_mix(proj, w_pool, pool_scale, sgu_ln_g, sgu_ln_b, sgu_w, sgu_b, seq):
    t = proj.shape[0]
    n_pool = len(POOL_WINDOWS)
    gd = w_pool.shape[-1]
    pool_w = n_pool * gd
    sgu_w_total = sgu_ln_g.shape[-1]
    head_dim = sgu_w_total // SGU_HEADS
    hps = gd // head_dim
    n_sgu = sgu_w_total // gd
    tm = _tile(seq, 512)
    seq_tiles = seq // tm
    v0 = (pool_w + sgu_w_total) // gd
    halo_per_tile = tm // POOL_HALO

    kern = functools.partial(_mix0_kernel, tm=tm, seq_tiles=seq_tiles, head_dim=head_dim)
    return pl.pallas_call(
        kern,
        grid=(t // tm, n_pool + n_sgu),
        in_specs=[
            pl.BlockSpec((tm, gd), lambda i, j: (i, j)),
            pl.BlockSpec((tm, gd), lambda i, j: (i, jnp.maximum(j - n_pool, 0) + v0)),
            pl.BlockSpec((POOL_HALO, gd),
                         lambda i, j: (jnp.maximum(i * halo_per_tile - 1, 0), jnp.minimum(j, n_pool - 1))),
            pl.BlockSpec((None, gd, gd), lambda i, j: (jnp.minimum(j, n_pool - 1), 0, 0)),
            pl.BlockSpec((1, gd), lambda i, j: (0, jnp.minimum(j, n_pool - 1))),
            pl.BlockSpec((1, gd), lambda i, j: (0, jnp.maximum(j - n_pool, 0))),
            pl.BlockSpec((1, gd), lambda i, j: (0, jnp.maximum(j - n_pool, 0))),
            pl.BlockSpec((hps, SGU_CHUNK, SGU_CHUNK), lambda i, j: (jnp.maximum(j - n_pool, 0), 0, 0)),
            pl.BlockSpec((hps, SGU_CHUNK, 1), lambda i, j: (jnp.maximum(j - n_pool, 0), 0, 0)),
        ],
        out_specs=pl.BlockSpec((tm, gd), lambda i, j: (i, j)),
        out_shape=jax.ShapeDtypeStruct((t, pool_w + sgu_w_total), BF16),
        scratch_shapes=[pltpu.VMEM((POOL_HALO + tm, gd), F32)],
        compiler_params=_params("parallel", "arbitrary"),
        name="pool_sgu_mix",
    )(proj, proj, proj, w_pool.astype(BF16), pool_scale.reshape(1, pool_w),
      sgu_ln_g.reshape(1, -1), sgu_ln_b.reshape(1, -1), sgu_w, sgu_b[:, :, None])


def _glu_kernel(a_ref, bv_ref, bg_ref, biasv_ref, biasg_ref, o_ref):
    a = a_ref[...]
    val = jnp.dot(a, bv_ref[...], preferred_element_type=F32) + biasv_ref[...]
    gate = jnp.dot(a, bg_ref[...], preferred_element_type=F32) + biasg_ref[...]
    res = val * jax.nn.sigmoid(gate)
    for lc in range(o_ref.shape[0]):
        o_ref[lc] = res[:, lc * LANES:(lc + 1) * LANES]


def _pointwise_glu(a, w, bias):
    m, k = a.shape
    c = w.shape[1] // 2
    tm, tn = _tile(m, 512), _tile(c, 512)
    gate0 = c // tn
    bias2 = bias.reshape(1, 2 * c)
    return pl.pallas_call(
        _glu_kernel,
        grid=(c // tn, m // tm),
        in_specs=[
            pl.BlockSpec((tm, k), lambda j, i: (i, 0)),
            pl.BlockSpec((k, tn), lambda j, i: (0, j)),
            pl.BlockSpec((k, tn), lambda j, i: (0, j + gate0)),
            pl.BlockSpec((1, tn), lambda j, i: (0, j)),
            pl.BlockSpec((1, tn), lambda j, i: (0, j + gate0)),
        ],
        out_specs=pl.BlockSpec((tn // LANES, tm, LANES), lambda j, i: (j, i, 0)),
        out_shape=jax.ShapeDtypeStruct((c // LANES, m, LANES), F32),
        compiler_params=_params("parallel", "parallel"),
        name="pointwise_glu",
    )(a, w, w, bias2, bias2)


def _conv_kernel(g_ref, halo_ref, w_ref, db_ref, lg_ref, lb_ref, o_ref, ext_ref, cv_ref,
                 *, ts, seq_tiles, channels):
    i = pl.program_id(0)
    n_lc = g_ref.shape[0]
    halo = halo_ref[...]
    ext_ref[:, 0:CONV_HALO, :] = jnp.where(i % seq_tiles == 0, jnp.zeros_like(halo), halo)
    ext_ref[:, CONV_HALO:CONV_HALO + ts, :] = g_ref[...]
    first_tap_row = CONV_HALO - (CONV_TAPS - 1)

    def conv_column(lc, row_sum):
        acc = jnp.zeros((ts, LANES), F32)
        for tap in range(CONV_TAPS):
            acc = acc + w_ref[lc, tap:tap + 1, :] * ext_ref[lc, pl.ds(first_tap_row + tap, ts), :]
        conv = acc + db_ref[lc]
        cv_ref[lc] = conv
        return row_sum + conv

    row_sum = lax.fori_loop(0, n_lc, conv_column, jnp.zeros((ts, LANES), F32))
    mu = jnp.sum(row_sum, axis=-1, keepdims=True) / channels

    def sq_column(lc, sq_sum):
        d = cv_ref[lc] - mu
        return sq_sum + d * d

    sq_sum = lax.fori_loop(0, n_lc, sq_column, jnp.zeros((ts, LANES), F32))
    rstd = lax.rsqrt(jnp.sum(sq_sum, axis=-1, keepdims=True) / channels + LN_EPS)
    for lc in range(n_lc):
        y = (cv_ref[lc] - mu) * rstd * lg_ref[lc] + lb_ref[lc]
        o_ref[:, lc * LANES:(lc + 1) * LANES] = (y * jax.nn.sigmoid(y)).astype(o_ref.dtype)


def _lane_columns(v):
    return v.reshape(-1, 1, LANES)


def _conv_ln_swish(g3, dw_w, dw_b, ln_g, ln_b, seq):
    n_lc, t, _ = g3.shape
    channels = n_lc * LANES
    ts = _tile(seq, 256)
    seq_tiles = seq // ts
    halo_per_tile = ts // CONV_HALO
    w3 = dw_w.reshape(CONV_TAPS, n_lc, LANES).transpose(1, 0, 2)
    vec = pl.BlockSpec((n_lc, 1, LANES), lambda i: (0, 0, 0))
    kern = functools.partial(_conv_kernel, ts=ts, seq_tiles=seq_tiles, channels=channels)
    return pl.pallas_call(
        kern,
        grid=(t // ts,),
        in_specs=[
            pl.BlockSpec((n_lc, ts, LANES), lambda i: (0, i, 0)),
            pl.BlockSpec((n_lc, CONV_HALO, LANES), lambda i: (0, jnp.maximum(i * halo_per_tile - 1, 0), 0)),
            pl.BlockSpec((n_lc, CONV_TAPS, LANES), lambda i: (0, 0, 0)),
            vec, vec, vec,
        ],
        out_specs=pl.BlockSpec((ts, channels), lambda i: (i, 0)),
        out_shape=jax.ShapeDtypeStruct((t, channels), BF16),
        scratch_shapes=[pltpu.VMEM((n_lc, CONV_HALO + ts, LANES), F32),
                        pltpu.VMEM((n_lc, ts, LANES), F32)],
        compiler_params=_params("parallel"),
        name="conv_ln_swish",
    )(g3, g3, w3, _lane_columns(dw_b), _lane_columns(ln_g), _lane_columns(ln_b))


def _split_bf16(x):
    hi = x.astype(BF16)
    lo = (x - hi.astype(F32)).astype(BF16)
    return hi, lo


def _rank_desc(vals, n):
    rows = lax.broadcasted_iota(jnp.int32, vals.shape, 0)
    rank = jnp.zeros(vals.shape, jnp.int32)
    for other in range(n):
        v = vals[other:other + 1, :]
        ahead = (v > vals) | ((v == vals) & (other < rows))
        rank = rank + ahead.astype(jnp.int32)
    return rank


def _ln_router_kernel(h_ref, m_ref, g_ref, b_ref, wt_ref, br_ref,
                      o32_ref, o16_ref, e_ref, pos_ref, gate_ref, cnt_ref, carry_ref, *, tm, alpha):
    i = pl.program_id(0)

    @pl.when(i == 0)
    def _():
        carry_ref[...] = jnp.zeros_like(carry_ref)

    h = _layer_norm_rows(alpha * h_ref[...] + m_ref[...], g_ref[...], b_ref[...])
    o32_ref[...] = h
    o16_ref[...] = h.astype(BF16)

    h_hi, h_lo = _split_bf16(h)
    w_hi, w_lo = _split_bf16(wt_ref[...])
    nt = (((1,), (1,)), ((), ()))
    logits = (lax.dot_general(w_hi, h_hi, nt, preferred_element_type=F32)
              + lax.dot_general(w_hi, h_lo, nt, preferred_element_type=F32)
              + lax.dot_general(w_lo, h_hi, nt, preferred_element_type=F32))
    scores = jax.nn.sigmoid(logits)
    choice = scores + br_ref[...]

    group_scores = []
    for grp in range(N_EXPERT_GROUPS):
        cg = choice[grp * GROUP_SIZE:(grp + 1) * GROUP_SIZE, :]
        m1 = jnp.max(cg, axis=0, keepdims=True)
        is_max = cg == m1
        n_max = jnp.sum(is_max.astype(F32), axis=0, keepdims=True)
        below = jnp.max(jnp.where(is_max, -jnp.inf, cg), axis=0, keepdims=True)
        group_scores.append(m1 + jnp.where(n_max >= 2.0, m1, below))
    group_score = jnp.concatenate(group_scores, axis=0)
    group_ok = _rank_desc(group_score, N_EXPERT_GROUPS) < TOPK_GROUPS
    masked = jnp.concatenate(
        [jnp.where(group_ok[grp:grp + 1, :], choice[grp * GROUP_SIZE:(grp + 1) * GROUP_SIZE, :], -jnp.inf)
         for grp in range(N_EXPERT_GROUPS)], axis=0)
    rank = _rank_desc(masked, N_EXPERTS)
    chosen = rank < TOP_K

    picked = jnp.where(chosen, scores, 0.0)
    gate = picked / jnp.sum(picked, axis=0, keepdims=True) * ROUTED_SCALE

    onehot = chosen.astype(F32)
    earlier = (lax.broadcasted_iota(jnp.int32, (tm, tm), 0)
               < lax.broadcasted_iota(jnp.int32, (tm, tm), 1)).astype(BF16)
    before = jnp.dot(onehot.astype(BF16), earlier, preferred_element_type=F32) + carry_ref[...]
    carry_ref[...] = carry_ref[...] + jnp.sum(onehot, axis=1, keepdims=True)
    cnt_ref[...] = jnp.broadcast_to(carry_ref[...], cnt_ref.shape).astype(jnp.int32)

    expert_id = lax.broadcasted_iota(jnp.int32, (N_EXPERTS, tm), 0).astype(F32)
    slot_row = lax.broadcasted_iota(jnp.int32, (SUBLANES, tm), 0)
    e_out = jnp.zeros((SUBLANES, tm), F32)
    pos_out = jnp.zeros((SUBLANES, tm), F32)
    gate_out = jnp.zeros((SUBLANES, tm), F32)
    for slot in range(TOP_K):
        here = rank == slot
        e_out = jnp.where(slot_row == slot, jnp.sum(jnp.where(here, expert_id, 0.0), axis=0, keepdims=True), e_out)
        pos_out = jnp.where(slot_row == slot, jnp.sum(jnp.where(here, before, 0.0), axis=0, keepdims=True), pos_out)
        gate_out = jnp.where(slot_row == slot, jnp.sum(jnp.where(here, gate, 0.0), axis=0, keepdims=True), gate_out)
    e_ref[...] = e_out.astype(jnp.int32)
    pos_ref[...] = pos_out.astype(jnp.int32)
    gate_ref[...] = gate_out


def _residual_ln_route(h, mix, g, b, w_router, b_router, alpha):
    t, d = h.shape
    tm = _tile(t, 256)
    row = pl.BlockSpec((tm, d), lambda i: (i, 0))
    vec = pl.BlockSpec((1, d), lambda i: (0, 0))
    slot_rows = pl.BlockSpec((SUBLANES, tm), lambda i: (0, i))
    return pl.pallas_call(
        functools.partial(_ln_router_kernel, tm=tm, alpha=alpha),
        grid=(t // tm,),
        in_specs=[row, row, vec, vec,
                  pl.BlockSpec((N_EXPERTS, d), lambda i: (0, 0)),
                  pl.BlockSpec((N_EXPERTS, 1), lambda i: (0, 0))],
        out_specs=[row, row, slot_rows, slot_rows, slot_rows,
                   pl.BlockSpec((N_EXPERTS, LANES), lambda i: (0, 0))],
        out_shape=[jax.ShapeDtypeStruct((t, d), F32),
                   jax.ShapeDtypeStruct((t, d), BF16),
                   jax.ShapeDtypeStruct((SUBLANES, t), jnp.int32),
                   jax.ShapeDtypeStruct((SUBLANES, t), jnp.int32),
                   jax.ShapeDtypeStruct((SUBLANES, t), F32),
                   jax.ShapeDtypeStruct((N_EXPERTS, LANES), jnp.int32)],
        scratch_shapes=[pltpu.VMEM((N_EXPERTS, 1), F32)],
        compiler_params=_params("arbitrary"),
        name="residual_ln_route",
    )(h, mix, g.reshape(1, d), b.reshape(1, d), w_router.T, b_router.reshape(N_EXPERTS, 1))


HIGH_HALF = 0xFFFF0000
ROW_DMA_UNROLL = 4


def _pack_bf16_pairs(x16):
    half = x16.shape[1] // 2
    lo = lax.bitcast_convert_type(x16[:, :half].astype(F32), jnp.uint32) >> 16
    hi = lax.bitcast_convert_type(x16[:, half:].astype(F32), jnp.uint32) & jnp.uint32(HIGH_HALF)
    return lo | hi


def _unpack_bf16_pairs(words):
    lo = lax.bitcast_convert_type(words << 16, F32)
    hi = lax.bitcast_convert_type(words & jnp.uint32(HIGH_HALF), F32)
    return jnp.concatenate([lo, hi], axis=1).astype(BF16)


def _dispatch_kernel(pend_ref, padded_ref, nu_ref, dest_ref, h16_ref, xs_ref, h_ref, zero_ref, sem, zsem, *, tm):
    i = pl.program_id(0)
    n_blocks = xs_ref.shape[0] // EXPERT_BLOCK
    h_ref[...] = _pack_bf16_pairs(h16_ref[...])

    def zero_copy(start):
        start = pl.multiple_of(start, EXPERT_BLOCK)
        return pltpu.make_async_copy(zero_ref, xs_ref.at[pl.ds(start, EXPERT_BLOCK), :], zsem)

    def tail_zero_copy(e):
        return zero_copy(pend_ref[e] - EXPERT_BLOCK)

    def start_unused(b, carry):
        zero_copy(b * EXPERT_BLOCK).start()
        return carry

    def wait_unused(b, carry):
        zero_copy(b * EXPERT_BLOCK).wait()
        return carry

    @pl.when(i == 0)
    def _():
        zero_ref[...] = jnp.zeros_like(zero_ref)
        for e in range(N_EXPERTS):
            @pl.when(padded_ref[e] > 0)
            def _(e=e):
                tail_zero_copy(e).start()
        lax.fori_loop(nu_ref[0], n_blocks, start_unused, 0)
        for e in range(N_EXPERTS):
            @pl.when(padded_ref[e] > 0)
            def _(e=e):
                tail_zero_copy(e).wait()
        lax.fori_loop(nu_ref[0], n_blocks, wait_unused, 0)

    def row_copy(t, k):
        return pltpu.make_async_copy(h_ref.at[pl.ds(t, 1), :], xs_ref.at[pl.ds(dest_ref[k, t], 1), :], sem)

    def start_rows(t, carry):
        for k in range(TOP_K):
            row_copy(t, k).start()
        return carry

    def wait_rows(t, carry):
        for k in range(TOP_K):
            row_copy(t, k).wait()
        return carry

    lax.fori_loop(0, tm, start_rows, 0, unroll=ROW_DMA_UNROLL)
    lax.fori_loop(0, tm, wait_rows, 0, unroll=ROW_DMA_UNROLL)


def _dispatch(h16, dest, pad_ends, padded, n_used, n_rows):
    t, d = h16.shape
    tm = _tile(t, LANES)
    words = d // 2
    return pl.pallas_call(
        functools.partial(_dispatch_kernel, tm=tm),
        grid_spec=pltpu.PrefetchScalarGridSpec(
            num_scalar_prefetch=3,
            grid=(t // tm,),
            in_specs=[pl.BlockSpec((SUBLANES, tm), lambda i, pe, pd, nu: (0, i), memory_space=pltpu.SMEM),
                      pl.BlockSpec((tm, d), lambda i, pe, pd, nu: (i, 0))],
            out_specs=pl.BlockSpec(memory_space=pl.ANY),
            scratch_shapes=[pltpu.VMEM((tm, words), jnp.uint32),
                            pltpu.VMEM((EXPERT_BLOCK, words), jnp.uint32),
                            pltpu.SemaphoreType.DMA(()), pltpu.SemaphoreType.DMA(())],
        ),
        out_shape=jax.ShapeDtypeStruct((n_rows, words), jnp.uint32),
        compiler_params=_params("arbitrary"),
        name="dispatch",
    )(pad_ends, padded, n_used, dest, h16)


CAST_CHUNK_ELEMS = 256 * 1024


def _cast_rows_to_bf16(src_ref, dst_ref):
    chunk = _tile(src_ref.shape[0], CAST_CHUNK_ELEMS // src_ref.shape[1])

    def body(i, carry):
        r = pl.multiple_of(i * chunk, chunk)
        dst_ref[pl.ds(r, chunk), :] = src_ref[pl.ds(r, chunk), :].astype(BF16)
        return carry

    lax.fori_loop(0, src_ref.shape[0] // chunk, body, 0)


def _expert_kernel(be_ref, first_ref, next_ref, nu_ref, x_ref, wgu_hbm, wd_hbm, y_ref,
                   stage_gu_ref, stage_d_ref, wgu_ref, wd_ref, sem, *, packed, pack_out):
    b = pl.program_id(0)
    used = b < nu_ref[0]

    def weight_copies(e):
        return (pltpu.make_async_copy(wgu_hbm.at[e], stage_gu_ref, sem.at[0]),
                pltpu.make_async_copy(wd_hbm.at[e], stage_d_ref, sem.at[1]))

    @pl.when(jnp.logical_and(used, first_ref[b] == 1))
    def _():
        @pl.when(b == 0)
        def _():
            for cp in weight_copies(be_ref[0]):
                cp.start()
        for cp in weight_copies(be_ref[b]):
            cp.wait()
        _cast_rows_to_bf16(stage_gu_ref, wgu_ref)
        _cast_rows_to_bf16(stage_d_ref, wd_ref)

        @pl.when(next_ref[b] >= 0)
        def _():
            for cp in weight_copies(next_ref[b]):
                cp.start()

    @pl.when(used)
    def _():
        f = wd_ref.shape[0]
        x = _unpack_bf16_pairs(x_ref[...]) if packed else x_ref[...]
        gu = jnp.dot(x, wgu_ref[...], preferred_element_type=F32)
        gate, up = gu[:, :f], gu[:, f:]
        act = (gate * jax.nn.sigmoid(gate)) * up
        y = jnp.dot(act.astype(BF16), wd_ref[...], preferred_element_type=F32)
        y_ref[...] = _pack_bf16_pairs(y.astype(BF16)) if pack_out else y

    @pl.when(jnp.logical_not(used))
    def _():
        y_ref[...] = jnp.zeros_like(y_ref)


def _grouped_swiglu(x, w_gu, w_down, block_expert, block_first, block_next, n_used, tm, pack_out):
    rows = x.shape[0]
    _, d, f2 = w_gu.shape
    f = w_down.shape[1]
    packed = x.dtype == jnp.uint32
    out_cols, out_dtype = (d // 2, jnp.uint32) if pack_out else (d, F32)

    def row_block(b, be, bf, bn, nu):
        return (jnp.minimum(b, nu[0] - 1), 0)

    return pl.pallas_call(
        functools.partial(_expert_kernel, packed=packed, pack_out=pack_out),
        grid_spec=pltpu.PrefetchScalarGridSpec(
            num_scalar_prefetch=4,
            grid=(rows // tm,),
            in_specs=[pl.BlockSpec((tm, x.shape[1]), row_block),
                      pl.BlockSpec(memory_space=pl.ANY),
                      pl.BlockSpec(memory_space=pl.ANY)],
            out_specs=pl.BlockSpec((tm, out_cols), lambda b, be, bf, bn, nu: (b, 0)),
            scratch_shapes=[pltpu.VMEM((d, f2), F32), pltpu.VMEM((f, d), F32),
                            pltpu.VMEM((d, f2), BF16), pltpu.VMEM((f, d), BF16),
                            pltpu.SemaphoreType.DMA((2,))],
        ),
        out_shape=jax.ShapeDtypeStruct((rows, out_cols), out_dtype),
        compiler_params=_params("arbitrary"),
        name="grouped_swiglu",
    )(block_expert, block_first, block_next, n_used, x, w_gu, w_down)


def _combine_kernel(dcur_ref, dnext_ref, gate_ref, sh_ref, h_ref, lg_ref, lb_ref, y_ref,
                    o32_ref, o16_ref, buf_ref, sem, *, tm, alpha):
    i = pl.program_id(0)
    slot = i % 2

    def row_copy(dref, t, k, s):
        return pltpu.make_async_copy(y_ref.at[pl.ds(dref[k, t], 1), :],
                                     buf_ref.at[s, k, pl.ds(t, 1), :], sem.at[s])

    def start_tile(dref, s):
        def body(t, carry):
            for k in range(TOP_K):
                row_copy(dref, t, k, s).start()
            return carry
        lax.fori_loop(0, tm, body, 0, unroll=ROW_DMA_UNROLL)

    @pl.when(i == 0)
    def _():
        start_tile(dcur_ref, 0)

    @pl.when(i + 1 < pl.num_programs(0))
    def _():
        start_tile(dnext_ref, 1 - slot)

    def wait_body(t, carry):
        for k in range(TOP_K):
            row_copy(dcur_ref, t, k, slot).wait()
        return carry
    lax.fori_loop(0, tm, wait_body, 0, unroll=ROW_DMA_UNROLL)

    half = h_ref.shape[1] // 2
    base = alpha * h_ref[...] + sh_ref[...]
    acc_lo, acc_hi = base[:, :half], base[:, half:]
    for k in range(TOP_K):
        words = buf_ref[slot, k]
        gate = gate_ref[:, k:k + 1]
        acc_lo = acc_lo + gate * lax.bitcast_convert_type(words << 16, F32)
        acc_hi = acc_hi + gate * lax.bitcast_convert_type(words & jnp.uint32(HIGH_HALF), F32)
    y = _layer_norm_rows(jnp.concatenate([acc_lo, acc_hi], axis=1), lg_ref[...], lb_ref[...])
    o32_ref[...] = y
    o16_ref[...] = y.astype(BF16)


def _combine_ln(dest, gates_t, shared, h, y, g, b, alpha):
    t, d = h.shape
    tm = _tile(t, LANES)
    n_tiles = t // tm
    row = pl.BlockSpec((tm, d), lambda i: (i, 0))
    vec = pl.BlockSpec((1, d), lambda i: (0, 0))
    return pl.pallas_call(
        functools.partial(_combine_kernel, tm=tm, alpha=alpha),
        grid=(n_tiles,),
        in_specs=[pl.BlockSpec((SUBLANES, tm), lambda i: (0, i), memory_space=pltpu.SMEM),
                  pl.BlockSpec((SUBLANES, tm), lambda i: (0, jnp.minimum(i + 1, n_tiles - 1)),
                               memory_space=pltpu.SMEM),
                  pl.BlockSpec((tm, SUBLANES), lambda i: (i, 0)),
                  row, row, vec, vec,
                  pl.BlockSpec(memory_space=pl.ANY)],
        out_specs=[row, row],
        out_shape=[jax.ShapeDtypeStruct((t, d), F32), jax.ShapeDtypeStruct((t, d), BF16)],
        scratch_shapes=[pltpu.VMEM((2, TOP_K, tm, d // 2), jnp.uint32), pltpu.SemaphoreType.DMA((2,))],
        compiler_params=_params("arbitrary"),
        name="combine_ln",
    )(dest, dest, gates_t, shared, h, g.reshape(1, d), b.reshape(1, d), y)


def _table_lookup(table, idx):
    n = table.shape[0]
    hit = idx[None] == jnp.arange(n, dtype=idx.dtype).reshape((n,) + (1,) * idx.ndim)
    return jnp.sum(jnp.where(hit, table.reshape((n,) + (1,) * idx.ndim), 0), axis=0)


def _moe_ffn_ln(h32, h16, routing, layer, w_gu, w_down, w_sh_gu, w_sh_down, ln_g, ln_b, alpha):
    t, d = h32.shape
    blk = EXPERT_BLOCK
    n_blocks = -(-(t * TOP_K) // blk) + N_EXPERTS
    n_rows = n_blocks * blk
    i32 = jnp.int32

    e_k, pos_k, gate_k, counts = routing
    counts = counts[:, 0]
    padded = (counts + blk - 1) // blk * blk
    pad_ends = jnp.cumsum(padded).astype(i32)
    pad_starts = pad_ends - padded
    dest = _table_lookup(pad_starts, e_k) + pos_k
    n_used = (pad_ends[-1:] // blk).astype(i32)
    block_id = jnp.arange(n_blocks, dtype=i32)
    block_expert = jnp.minimum(jnp.sum(pad_ends[None, :] <= (block_id * blk)[:, None], axis=1),
                               N_EXPERTS - 1).astype(i32)
    last_expert = jnp.sum(jnp.where(block_id == n_used[0] - 1, block_expert, 0))
    block_expert = jnp.where(block_id < n_used[0], block_expert, last_expert)
    block_first = jnp.logical_or(block_id == 0, block_expert != jnp.roll(block_expert, 1)).astype(i32)
    expert_id = jnp.arange(N_EXPERTS, dtype=i32)
    owner = jnp.where(padded > 0, expert_id, N_EXPERTS)
    at_or_after = lax.cummin(owner[::-1])[::-1]
    after = jnp.concatenate([at_or_after[1:], jnp.full((1,), N_EXPERTS, i32)])
    first_row = layer * N_EXPERTS
    next_expert = jnp.where(after < N_EXPERTS, after + first_row, -1).astype(i32)
    block_next = _table_lookup(next_expert, block_expert)

    xs = _dispatch(h16, dest, pad_ends, padded.astype(i32), n_used, n_rows)
    y = _grouped_swiglu(xs, w_gu.reshape((-1,) + w_gu.shape[2:]), w_down.reshape((-1,) + w_down.shape[2:]),
                        block_expert + first_row, block_first, block_next, n_used, blk, pack_out=True)

    n_sh = t // blk
    shared = _grouped_swiglu(h16, w_sh_gu, w_sh_down,
                             jnp.full((n_sh,), layer, i32), (jnp.arange(n_sh) == 0).astype(i32),
                             jnp.full((n_sh,), -1, i32), jnp.full((1,), n_sh, i32), blk, pack_out=False)
    return _combine_ln(dest, gate_k.T, shared, h32, y, ln_g, ln_b, alpha)


def kernel(x, w_in_ab, w_pool, pool_scale, sgu_ln_g, sgu_ln_b, sgu_w, sgu_b, w_out_ab, w_pw1, b_pw1, dw_w, dw_b, conv_ln_g, conv_ln_b, w_pw2, b_pw2, mix_ln_g, mix_ln_b, w_router, b_router, w_gu, w_down, w_shared_gu, w_shared_down, ffn_ln_g, ffn_ln_b):
    bsz, seq, d = x.shape
    depth = mix_ln_g.shape[0]
    alpha = (2 * depth) ** 0.25
    h32 = x.reshape(bsz * seq, d)
    h16 = h32.astype(BF16)
    for layer in range(depth):
        i = layer // 2
        if layer % 2 == 0:
            proj = _matmul(h16, w_in_ab[i].astype(BF16))
            ab = _pool_sgu_mix(proj, w_pool[i], pool_scale[i], sgu_ln_g[i], sgu_ln_b[i], sgu_w[i], sgu_b[i], seq)
            mix = _matmul(ab, w_out_ab[i].astype(BF16))
        else:
            g3 = _pointwise_glu(h16, w_pw1[i].astype(BF16), b_pw1[i])
            cb = _conv_ln_swish(g3, dw_w[i], dw_b[i], conv_ln_g[i], conv_ln_b[i], seq)
            mix = _matmul(cb, w_pw2[i].astype(BF16), bias=b_pw2[i])
        h32, h16, *routing = _residual_ln_route(h32, mix, mix_ln_g[layer], mix_ln_b[layer],
                                                w_router[layer], b_router[layer], alpha)
        h32, h16 = _moe_ffn_ln(h32, h16, routing, layer, w_gu, w_down, w_shared_gu, w_shared_down,
                               ffn_ln_g[layer], ffn_ln_b[layer], alpha)
    return h32.reshape(bsz, seq, d)
```

```python
import functools

import jax
import jax.numpy as jnp
from jax import lax
from jax.experimental import pallas as pl
from jax.experimental.pallas import tpu as pltpu

F32 = jnp.float32
BF16 = jnp.bfloat16

POOL_WINDOWS = (2, 4, 8, 16)
POOL_HALO = 16
SGU_HEADS = 8
SGU_CHUNK = 128
CONV_TAPS = 31
CONV_HALO = 32
N_EXPERTS = 64
TOP_K = 6
N_EXPERT_GROUPS = 8
GROUP_SIZE = N_EXPERTS // N_EXPERT_GROUPS
TOPK_GROUPS = 4
ROUTED_SCALE = 2.5
EXPERT_BLOCK = 256
LN_EPS = 1e-5

LANES = 128
SUBLANES = 8
VMEM_LIMIT_BYTES = 56 * 1024 * 1024


def _params(*semantics):
    return pltpu.CompilerParams(dimension_semantics=semantics, vmem_limit_bytes=VMEM_LIMIT_BYTES)


def _tile(n, pref):
    t = min(n, pref)
    while n % t:
        t //= 2
    return t


def _layer_norm_rows(z, g, b):
    mu = jnp.mean(z, axis=-1, keepdims=True)
    zc = z - mu
    var = jnp.mean(zc * zc, axis=-1, keepdims=True)
    return zc * lax.rsqrt(var + LN_EPS) * g + b


def _mm_kernel(a_ref, b_ref, o_ref):
    o_ref[...] = jnp.dot(a_ref[...], b_ref[...], preferred_element_type=F32).astype(o_ref.dtype)


def _mm_bias_kernel(a_ref, b_ref, bias_ref, o_ref):
    acc = jnp.dot(a_ref[...], b_ref[...], preferred_element_type=F32)
    o_ref[...] = (acc + bias_ref[...]).astype(o_ref.dtype)


def _matmul(a, b, bias=None, out_dtype=F32):
    m, k = a.shape
    n = b.shape[1]
    tm, tn = _tile(m, 512), _tile(n, 1024)
    in_specs = [pl.BlockSpec((tm, k), lambda j, i: (i, 0)),
                pl.BlockSpec((k, tn), lambda j, i: (0, j))]
    args = [a, b]
    body = _mm_kernel
    if bias is not None:
        in_specs.append(pl.BlockSpec((1, tn), lambda j, i: (0, j)))
        args.append(bias.reshape(1, n))
        body = _mm_bias_kernel
    return pl.pallas_call(
        body,
        grid=(n // tn, m // tm),
        in_specs=in_specs,
        out_specs=pl.BlockSpec((tm, tn), lambda j, i: (i, j)),
        out_shape=jax.ShapeDtypeStruct((m, n), out_dtype),
        compiler_params=_params("parallel", "parallel"),
        name="dense_matmul",
    )(*args)


def _mix0_kernel(pa_ref, pv_ref, halo_ref, wp_ref, ps_ref, lg_ref, lb_ref, sw_ref, sb_ref,
                 o_ref, ext_ref, *, tm, seq_tiles, head_dim):
    i = pl.program_id(0)
    j = pl.program_id(1)
    n_pool = len(POOL_WINDOWS)
    seq_tile = i % seq_tiles

    for grp, window in enumerate(POOL_WINDOWS):
        @pl.when(j == grp)
        def _(window=window):
            a = pa_ref[...]
            halo = halo_ref[...]
            ext_ref[0:POOL_HALO, :] = jnp.where(seq_tile == 0, jnp.zeros_like(halo), halo)
            ext_ref[POOL_HALO:POOL_HALO + tm, :] = a
            wsum = a
            for back in range(1, window):
                wsum = wsum + ext_ref[pl.ds(POOL_HALO - back, tm), :]
            pos = seq_tile * tm + lax.broadcasted_iota(jnp.int32, (tm, 1), 0)
            count = jnp.minimum(pos + 1, window).astype(F32)
            pooled = wsum / count - a
            mapped = jnp.dot(pooled.astype(BF16), wp_ref[...], preferred_element_type=F32)
            o_ref[...] = (mapped * ps_ref[...]).astype(o_ref.dtype)

    @pl.when(j >= n_pool)
    def _():
        u = jax.nn.gelu(pa_ref[...])
        v = jax.nn.gelu(pv_ref[...])
        r = lax.broadcasted_iota(jnp.int32, (SGU_CHUNK, SGU_CHUNK), 0)
        c = lax.broadcasted_iota(jnp.int32, (SGU_CHUNK, SGU_CHUNK), 1)
        for hh in range(pa_ref.shape[1] // head_dim):
            cols = slice(hh * head_dim, (hh + 1) * head_dim)
            vn = _layer_norm_rows(v[:, cols], lg_ref[:, cols], lb_ref[:, cols]).astype(BF16)
            w_s = jnp.where(r >= c, sw_ref[hh], 0.0).astype(BF16)
            bias = sb_ref[hh]
            for ch in range(tm // SGU_CHUNK):
                rows = slice(ch * SGU_CHUNK, (ch + 1) * SGU_CHUNK)
                sv = jnp.dot(w_s, vn[rows], preferred_element_type=F32) + bias
                o_ref[rows, cols] = (u[rows, cols] * sv).astype(o_ref.dtype)


def _pool_sgu_mix(proj, w_pool, pool_scale, sgu_ln_g, sgu_ln_b, sgu_w, sgu_b, seq):
    t = proj.shape[0]
    n_pool = len(POOL_WINDOWS)
    gd = w_pool.shape[-1]
    pool_w = n_pool * gd
    sgu_w_total = sgu_ln_g.shape[-1]
    head_dim = sgu_w_total // SGU_HEADS
    hps = gd // head_dim
    n_sgu = sgu_w_total // gd
    tm = _tile(seq, 512)
    seq_tiles = seq // tm
    v0 = (pool_w + sgu_w_total) // gd
    halo_per_tile = tm // POOL_HALO

    kern = functools.partial(_mix0_kernel, tm=tm, seq_tiles=seq_tiles, head_dim=head_dim)
    return pl.pallas_call(
        kern,
        grid=(t // tm, n_pool + n_sgu),
        in_specs=[
            pl.BlockSpec((tm, gd), lambda i, j: (i, j)),
            pl.BlockSpec((tm, gd), lambda i, j: (i, jnp.maximum(j - n_pool, 0) + v0)),
            pl.BlockSpec((POOL_HALO, gd),
                         lambda i, j: (jnp.maximum(i * halo_per_tile - 1, 0), jnp.minimum(j, n_pool - 1))),
            pl.BlockSpec((None, gd, gd), lambda i, j: (jnp.minimum(j, n_pool - 1), 0, 0)),
            pl.BlockSpec((1, gd), lambda i, j: (0, jnp.minimum(j, n_pool - 1))),
            pl.BlockSpec((1, gd), lambda i, j: (0, jnp.maximum(j - n_pool, 0))),
            pl.BlockSpec((1, gd), lambda i, j: (0, jnp.maximum(j - n_pool, 0))),
            pl.BlockSpec((hps, SGU_CHUNK, SGU_CHUNK), lambda i, j: (jnp.maximum(j - n_pool, 0), 0, 0)),
            pl.BlockSpec((hps, SGU_CHUNK, 1), lambda i, j: (jnp.maximum(j - n_pool, 0), 0, 0)),
        ],
        out_specs=pl.BlockSpec((tm, gd), lambda i, j: (i, j)),
        out_shape=jax.ShapeDtypeStruct((t, pool_w + sgu_w_total), BF16),
        scratch_shapes=[pltpu.VMEM((POOL_HALO + tm, gd), F32)],
        compiler_params=_params("parallel", "arbitrary"),
        name="pool_sgu_mix",
    )(proj, proj, proj, w_pool.astype(BF16), pool_scale.reshape(1, pool_w),
      sgu_ln_g.reshape(1, -1), sgu_ln_b.reshape(1, -1), sgu_w, sgu_b[:, :, None])


def _glu_kernel(a_ref, bv_ref, bg_ref, biasv_ref, biasg_ref, o_ref):
    a = a_ref[...]
    val = jnp.dot(a, bv_ref[...], preferred_element_type=F32) + biasv_ref[...]
    gate = jnp.dot(a, bg_ref[...], preferred_element_type=F32) + biasg_ref[...]
    res = val * jax.nn.sigmoid(gate)
    for lc in range(o_ref.shape[0]):
        o_ref[lc] = res[:, lc * LANES:(lc + 1) * LANES]


def _pointwise_glu(a, w, bias):
    m, k = a.shape
    c = w.shape[1] // 2
    tm, tn = _tile(m, 512), _tile(c, 512)
    gate0 = c // tn
    bias2 = bias.reshape(1, 2 * c)
    return pl.pallas_call(
        _glu_kernel,
        grid=(c // tn, m // tm),
        in_specs=[
            pl.BlockSpec((tm, k), lambda j, i: (i, 0)),
            pl.BlockSpec((k, tn), lambda j, i: (0, j)),
            pl.BlockSpec((k, tn), lambda j, i: (0, j + gate0)),
            pl.BlockSpec((1, tn), lambda j, i: (0, j)),
            pl.BlockSpec((1, tn), lambda j, i: (0, j + gate0)),
        ],
        out_specs=pl.BlockSpec((tn // LANES, tm, LANES), lambda j, i: (j, i, 0)),
        out_shape=jax.ShapeDtypeStruct((c // LANES, m, LANES), F32),
        compiler_params=_params("parallel", "parallel"),
        name="pointwise_glu",
    )(a, w, w, bias2, bias2)


def _conv_kernel(g_ref, halo_ref, w_ref, db_ref, lg_ref, lb_ref, o_ref, ext_ref, cv_ref,
                 *, ts, seq_tiles, channels):
    i = pl.program_id(0)
    n_lc = g_ref.shape[0]
    halo = halo_ref[...]
    ext_ref[:, 0:CONV_HALO, :] = jnp.where(i % seq_tiles == 0, jnp.zeros_like(halo), halo)
    ext_ref[:, CONV_HALO:CONV_HALO + ts, :] = g_ref[...]
    first_tap_row = CONV_HALO - (CONV_TAPS - 1)

    def conv_column(lc, row_sum):
        acc = jnp.zeros((ts, LANES), F32)
        for tap in range(CONV_TAPS):
            acc = acc + w_ref[lc, tap:tap + 1, :] * ext_ref[lc, pl.ds(first_tap_row + tap, ts), :]
        conv = acc + db_ref[lc]
        cv_ref[lc] = conv
        return row_sum + conv

    row_sum = lax.fori_loop(0, n_lc, conv_column, jnp.zeros((ts, LANES), F32))
    mu = jnp.sum(row_sum, axis=-1, keepdims=True) / channels

    def sq_column(lc, sq_sum):
        d = cv_ref[lc] - mu
        return sq_sum + d * d

    sq_sum = lax.fori_loop(0, n_lc, sq_column, jnp.zeros((ts, LANES), F32))
    rstd = lax.rsqrt(jnp.sum(sq_sum, axis=-1, keepdims=True) / channels + LN_EPS)
    for lc in range(n_lc):
        y = (cv_ref[lc] - mu) * rstd * lg_ref[lc] + lb_ref[lc]
        o_ref[:, lc * LANES:(lc + 1) * LANES] = (y * jax.nn.sigmoid(y)).astype(o_ref.dtype)


def _lane_columns(v):
    return v.reshape(-1, 1, LANES)


def _conv_ln_swish(g3, dw_w, dw_b, ln_g, ln_b, seq):
    n_lc, t, _ = g3.shape
    channels = n_lc * LANES
    ts = _tile(seq, 256)
    seq_tiles = seq // ts
    halo_per_tile = ts // CONV_HALO
    w3 = dw_w.reshape(CONV_TAPS, n_lc, LANES).transpose(1, 0, 2)
    vec = pl.BlockSpec((n_lc, 1, LANES), lambda i: (0, 0, 0))
    kern = functools.partial(_conv_kernel, ts=ts, seq_tiles=seq_tiles, channels=channels)
    return pl.pallas_call(
        kern,
        grid=(t // ts,),
        in_specs=[
            pl.BlockSpec((n_lc, ts, LANES), lambda i: (0, i, 0)),
            pl.BlockSpec((n_lc, CONV_HALO, LANES), lambda i: (0, jnp.maximum(i * halo_per_tile - 1, 0), 0)),
            pl.BlockSpec((n_lc, CONV_TAPS, LANES), lambda i: (0, 0, 0)),
            vec, vec, vec,
        ],
        out_specs=pl.BlockSpec((ts, channels), lambda i: (i, 0)),
        out_shape=jax.ShapeDtypeStruct((t, channels), BF16),
        scratch_shapes=[pltpu.VMEM((n_lc, CONV_HALO + ts, LANES), F32),
                        pltpu.VMEM((n_lc, ts, LANES), F32)],
        compiler_params=_params("parallel"),
        name="conv_ln_swish",
    )(g3, g3, w3, _lane_columns(dw_b), _lane_columns(ln_g), _lane_columns(ln_b))


def _split_bf16(x):
    hi = x.astype(BF16)
    lo = (x - hi.astype(F32)).astype(BF16)
    return hi, lo


def _rank_desc(vals, n):
    rows = lax.broadcasted_iota(jnp.int32, vals.shape, 0)
    rank = jnp.zeros(vals.shape, jnp.int32)
    for other in range(n):
        v = vals[other:other + 1, :]
        ahead = (v > vals) | ((v == vals) & (other < rows))
        rank = rank + ahead.astype(jnp.int32)
    return rank


def _ln_router_kernel(h_ref, m_ref, g_ref, b_ref, wt_ref, br_ref,
                      o32_ref, o16_ref, e_ref, pos_ref, gate_ref, cnt_ref, carry_ref, *, tm, alpha):
    i = pl.program_id(0)

    @pl.when(i == 0)
    def _():
        carry_ref[...] = jnp.zeros_like(carry_ref)

    h = _layer_norm_rows(alpha * h_ref[...] + m_ref[...], g_ref[...], b_ref[...])
    o32_ref[...] = h
    o16_ref[...] = h.astype(BF16)

    h_hi, h_lo = _split_bf16(h)
    w_hi, w_lo = _split_bf16(wt_ref[...])
    nt = (((1,), (1,)), ((), ()))
    logits = (lax.dot_general(w_hi, h_hi, nt, preferred_element_type=F32)
              + lax.dot_general(w_hi, h_lo, nt, preferred_element_type=F32)
              + lax.dot_general(w_lo, h_hi, nt, preferred_element_type=F32))
    scores = jax.nn.sigmoid(logits)
    choice = scores + br_ref[...]

    group_scores = []
    for grp in range(N_EXPERT_GROUPS):
        cg = choice[grp * GROUP_SIZE:(grp + 1) * GROUP_SIZE, :]
        m1 = jnp.max(cg, axis=0, keepdims=True)
        is_max = cg == m1
        n_max = jnp.sum(is_max.astype(F32), axis=0, keepdims=True)
        below = jnp.max(jnp.where(is_max, -jnp.inf, cg), axis=0, keepdims=True)
        group_scores.append(m1 + jnp.where(n_max >= 2.0, m1, below))
    group_score = jnp.concatenate(group_scores, axis=0)
    group_ok = _rank_desc(group_score, N_EXPERT_GROUPS) < TOPK_GROUPS
    masked = jnp.concatenate(
        [jnp.where(group_ok[grp:grp + 1, :], choice[grp * GROUP_SIZE:(grp + 1) * GROUP_SIZE, :], -jnp.inf)
         for grp in range(N_EXPERT_GROUPS)], axis=0)
    rank = _rank_desc(masked, N_EXPERTS)
    chosen = rank < TOP_K

    picked = jnp.where(chosen, scores, 0.0)
    gate = picked / jnp.sum(picked, axis=0, keepdims=True) * ROUTED_SCALE

    onehot = chosen.astype(F32)
    earlier = (lax.broadcasted_iota(jnp.int32, (tm, tm), 0)
               < lax.broadcasted_iota(jnp.int32, (tm, tm), 1)).astype(BF16)
    before = jnp.dot(onehot.astype(BF16), earlier, preferred_element_type=F32) + carry_ref[...]
    carry_ref[...] = carry_ref[...] + jnp.sum(onehot, axis=1, keepdims=True)
    cnt_ref[...] = jnp.broadcast_to(carry_ref[...], cnt_ref.shape).astype(jnp.int32)

    expert_id = lax.broadcasted_iota(jnp.int32, (N_EXPERTS, tm), 0).astype(F32)
    slot_row = lax.broadcasted_iota(jnp.int32, (SUBLANES, tm), 0)
    e_out = jnp.zeros((SUBLANES, tm), F32)
    pos_out = jnp.zeros((SUBLANES, tm), F32)
    gate_out = jnp.zeros((SUBLANES, tm), F32)
    for slot in range(TOP_K):
        here = rank == slot
        e_out = jnp.where(slot_row == slot, jnp.sum(jnp.where(here, expert_id, 0.0), axis=0, keepdims=True), e_out)
        pos_out = jnp.where(slot_row == slot, jnp.sum(jnp.where(here, before, 0.0), axis=0, keepdims=True), pos_out)
        gate_out = jnp.where(slot_row == slot, jnp.sum(jnp.where(here, gate, 0.0), axis=0, keepdims=True), gate_out)
    e_ref[...] = e_out.astype(jnp.int32)
    pos_ref[...] = pos_out.astype(jnp.int32)
    gate_ref[...] = gate_out


def _residual_ln_route(h, mix, g, b, w_router, b_router, alpha):
    t, d = h.shape
    tm = _tile(t, 256)
    row = pl.BlockSpec((tm, d), lambda i: (i, 0))
    vec = pl.BlockSpec((1, d), lambda i: (0, 0))
    slot_rows = pl.BlockSpec((SUBLANES, tm), lambda i: (0, i))
    return pl.pallas_call(
        functools.partial(_ln_router_kernel, tm=tm, alpha=alpha),
        grid=(t // tm,),
        in_specs=[row, row, vec, vec,
                  pl.BlockSpec((N_EXPERTS, d), lambda i: (0, 0)),
                  pl.BlockSpec((N_EXPERTS, 1), lambda i: (0, 0))],
        out_specs=[row, row, slot_rows, slot_rows, slot_rows,
                   pl.BlockSpec((N_EXPERTS, LANES), lambda i: (0, 0))],
        out_shape=[jax.ShapeDtypeStruct((t, d), F32),
                   jax.ShapeDtypeStruct((t, d), BF16),
                   jax.ShapeDtypeStruct((SUBLANES, t), jnp.int32),
                   jax.ShapeDtypeStruct((SUBLANES, t), jnp.int32),
                   jax.ShapeDtypeStruct((SUBLANES, t), F32),
                   jax.ShapeDtypeStruct((N_EXPERTS, LANES), jnp.int32)],
        scratch_shapes=[pltpu.VMEM((N_EXPERTS, 1), F32)],
        compiler_params=_params("arbitrary"),
        name="residual_ln_route",
    )(h, mix, g.reshape(1, d), b.reshape(1, d), w_router.T, b_router.reshape(N_EXPERTS, 1))


HIGH_HALF = 0xFFFF0000


def _pack_bf16_pairs(x16):
    half = x16.shape[1] // 2
    lo = lax.bitcast_convert_type(x16[:, :half].astype(F32), jnp.uint32) >> 16
    hi = lax.bitcast_convert_type(x16[:, half:].astype(F32), jnp.uint32) & jnp.uint32(HIGH_HALF)
    return lo | hi


def _unpack_bf16_pairs(words):
    lo = lax.bitcast_convert_type(words << 16, F32)
    hi = lax.bitcast_convert_type(words & jnp.uint32(HIGH_HALF), F32)
    return jnp.concatenate([lo, hi], axis=1).astype(BF16)


def _dispatch_kernel(pend_ref, padded_ref, nu_ref, dest_ref, h16_ref, xs_ref, h_ref, zero_ref, sem, zsem, *, tm):
    i = pl.program_id(0)
    n_blocks = xs_ref.shape[0] // EXPERT_BLOCK
    h_ref[...] = _pack_bf16_pairs(h16_ref[...])

    def zero_copy(start):
        start = pl.multiple_of(start, EXPERT_BLOCK)
        return pltpu.make_async_copy(zero_ref, xs_ref.at[pl.ds(start, EXPERT_BLOCK), :], zsem)

    def tail_zero_copy(e):
        return zero_copy(pend_ref[e] - EXPERT_BLOCK)

    def start_unused(b, carry):
        zero_copy(b * EXPERT_BLOCK).start()
        return carry

    def wait_unused(b, carry):
        zero_copy(b * EXPERT_BLOCK).wait()
        return carry

    @pl.when(i == 0)
    def _():
        zero_ref[...] = jnp.zeros_like(zero_ref)
        for e in range(N_EXPERTS):
            @pl.when(padded_ref[e] > 0)
            def _(e=e):
                tail_zero_copy(e).start()
        lax.fori_loop(nu_ref[0], n_blocks, start_unused, 0)
        for e in range(N_EXPERTS):
            @pl.when(padded_ref[e] > 0)
            def _(e=e):
                tail_zero_copy(e).wait()
        lax.fori_loop(nu_ref[0], n_blocks, wait_unused, 0)

    def row_copy(t, k):
        return pltpu.make_async_copy(h_ref.at[pl.ds(t, 1), :], xs_ref.at[pl.ds(dest_ref[k, t], 1), :], sem)

    def start_rows(grp, carry):
        for sub in range(SUBLANES):
            for k in range(TOP_K):
                row_copy(grp * SUBLANES + sub, k).start()
        return carry

    def wait_rows(grp, carry):
        for sub in range(SUBLANES):
            for k in range(TOP_K):
                row_copy(grp * SUBLANES + sub, k).wait()
        return carry

    lax.fori_loop(0, tm // SUBLANES, start_rows, 0)
    lax.fori_loop(0, tm // SUBLANES, wait_rows, 0)


def _dispatch(h16, dest, pad_ends, padded, n_used, n_rows):
    t, d = h16.shape
    tm = _tile(t, LANES)
    words = d // 2
    return pl.pallas_call(
        functools.partial(_dispatch_kernel, tm=tm),
        grid_spec=pltpu.PrefetchScalarGridSpec(
            num_scalar_prefetch=3,
            grid=(t // tm,),
            in_specs=[pl.BlockSpec((SUBLANES, tm), lambda i, pe, pd, nu: (0, i), memory_space=pltpu.SMEM),
                      pl.BlockSpec((tm, d), lambda i, pe, pd, nu: (i, 0))],
            out_specs=pl.BlockSpec(memory_space=pl.ANY),
            scratch_shapes=[pltpu.VMEM((tm, words), jnp.uint32),
                            pltpu.VMEM((EXPERT_BLOCK, words), jnp.uint32),
                            pltpu.SemaphoreType.DMA(()), pltpu.SemaphoreType.DMA(())],
        ),
        out_shape=jax.ShapeDtypeStruct((n_rows, words), jnp.uint32),
        compiler_params=_params("arbitrary"),
        name="dispatch",
    )(pad_ends, padded, n_used, dest, h16)


CAST_CHUNK_ELEMS = 256 * 1024


def _cast_rows_to_bf16(src_ref, dst_ref):
    chunk = _tile(src_ref.shape[0], CAST_CHUNK_ELEMS // src_ref.shape[1])

    def body(i, carry):
        r = pl.multiple_of(i * chunk, chunk)
        dst_ref[pl.ds(r, chunk), :] = src_ref[pl.ds(r, chunk), :].astype(BF16)
        return carry

    lax.fori_loop(0, src_ref.shape[0] // chunk, body, 0)


def _expert_kernel(be_ref, first_ref, next_ref, nu_ref, x_ref, wgu_hbm, wd_hbm, y_ref,
                   stage_gu_ref, stage_d_ref, wgu_ref, wd_ref, sem, *, packed, pack_out):
    b = pl.program_id(0)
    used = b < nu_ref[0]

    def weight_copies(e):
        return (pltpu.make_async_copy(wgu_hbm.at[e], stage_gu_ref, sem.at[0]),
                pltpu.make_async_copy(wd_hbm.at[e], stage_d_ref, sem.at[1]))

    @pl.when(jnp.logical_and(used, first_ref[b] == 1))
    def _():
        @pl.when(b == 0)
        def _():
            for cp in weight_copies(be_ref[0]):
                cp.start()
        for cp in weight_copies(be_ref[b]):
            cp.wait()
        _cast_rows_to_bf16(stage_gu_ref, wgu_ref)
        _cast_rows_to_bf16(stage_d_ref, wd_ref)

        @pl.when(next_ref[b] >= 0)
        def _():
            for cp in weight_copies(next_ref[b]):
                cp.start()

    @pl.when(used)
    def _():
        f = wd_ref.shape[0]
        x = _unpack_bf16_pairs(x_ref[...]) if packed else x_ref[...]
        gu = jnp.dot(x, wgu_ref[...], preferred_element_type=F32)
        gate, up = gu[:, :f], gu[:, f:]
        act = (gate * jax.nn.sigmoid(gate)) * up
        y = jnp.dot(act.astype(BF16), wd_ref[...], preferred_element_type=F32)
        y16 = y.astype(BF16)
        y_ref[...] = _pack_bf16_pairs(y16) if pack_out else y16

    @pl.when(jnp.logical_not(used))
    def _():
        y_ref[...] = jnp.zeros_like(y_ref)


def _grouped_swiglu(x, w_gu, w_down, block_expert, block_first, block_next, n_used, tm, pack_out):
    rows = x.shape[0]
    _, d, f2 = w_gu.shape
    f = w_down.shape[1]
    packed = x.dtype == jnp.uint32
    out_cols, out_dtype = (d // 2, jnp.uint32) if pack_out else (d, BF16)

    def row_block(b, be, bf, bn, nu):
        return (jnp.minimum(b, nu[0] - 1), 0)

    return pl.pallas_call(
        functools.partial(_expert_kernel, packed=packed, pack_out=pack_out),
        grid_spec=pltpu.PrefetchScalarGridSpec(
            num_scalar_prefetch=4,
            grid=(rows // tm,),
            in_specs=[pl.BlockSpec((tm, x.shape[1]), row_block),
                      pl.BlockSpec(memory_space=pl.ANY),
                      pl.BlockSpec(memory_space=pl.ANY)],
            out_specs=pl.BlockSpec((tm, out_cols), lambda b, be, bf, bn, nu: (b, 0)),
            scratch_shapes=[pltpu.VMEM((d, f2), F32), pltpu.VMEM((f, d), F32),
                            pltpu.VMEM((d, f2), BF16), pltpu.VMEM((f, d), BF16),
                            pltpu.SemaphoreType.DMA((2,))],
        ),
        out_shape=jax.ShapeDtypeStruct((rows, out_cols), out_dtype),
        compiler_params=_params("arbitrary"),
        name="grouped_swiglu",
    )(block_expert, block_first, block_next, n_used, x, w_gu, w_down)


def _combine_kernel(dcur_ref, dnext_ref, gate_ref, sh_ref, h_ref, lg_ref, lb_ref, y_ref,
                    o32_ref, o16_ref, buf_ref, sem, *, tm, alpha):
    i = pl.program_id(0)
    slot = i % 2

    def row_copy(dref, t, k, s):
        return pltpu.make_async_copy(y_ref.at[pl.ds(dref[k, t], 1), :],
                                     buf_ref.at[s, k, pl.ds(t, 1), :], sem.at[s])

    def start_tile(dref, s):
        def body(grp, carry):
            for sub in range(SUBLANES):
                for k in range(TOP_K):
                    row_copy(dref, grp * SUBLANES + sub, k, s).start()
            return carry
        lax.fori_loop(0, tm // SUBLANES, body, 0)

    @pl.when(i == 0)
    def _():
        start_tile(dcur_ref, 0)

    @pl.when(i + 1 < pl.num_programs(0))
    def _():
        start_tile(dnext_ref, 1 - slot)

    def wait_body(grp, carry):
        for sub in range(SUBLANES):
            for k in range(TOP_K):
                row_copy(dcur_ref, grp * SUBLANES + sub, k, slot).wait()
        return carry
    lax.fori_loop(0, tm // SUBLANES, wait_body, 0)

    half = h_ref.shape[1] // 2
    base = alpha * h_ref[...] + sh_ref[...].astype(F32)
    acc_lo, acc_hi = base[:, :half], base[:, half:]
    for k in range(TOP_K):
        words = buf_ref[slot, k]
        gate = gate_ref[:, k:k + 1]
        acc_lo = acc_lo + gate * lax.bitcast_convert_type(words << 16, F32)
        acc_hi = acc_hi + gate * lax.bitcast_convert_type(words & jnp.uint32(HIGH_HALF), F32)
    y = _layer_norm_rows(jnp.concatenate([acc_lo, acc_hi], axis=1), lg_ref[...], lb_ref[...])
    o32_ref[...] = y
    o16_ref[...] = y.astype(BF16)


def _combine_ln(dest, gates_t, shared, h, y, g, b, alpha):
    t, d = h.shape
    tm = _tile(t, LANES)
    n_tiles = t // tm
    row = pl.BlockSpec((tm, d), lambda i: (i, 0))
    vec = pl.BlockSpec((1, d), lambda i: (0, 0))
    return pl.pallas_call(
        functools.partial(_combine_kernel, tm=tm, alpha=alpha),
        grid=(n_tiles,),
        in_specs=[pl.BlockSpec((SUBLANES, tm), lambda i: (0, i), memory_space=pltpu.SMEM),
                  pl.BlockSpec((SUBLANES, tm), lambda i: (0, jnp.minimum(i + 1, n_tiles - 1)),
                               memory_space=pltpu.SMEM),
                  pl.BlockSpec((tm, SUBLANES), lambda i: (i, 0)),
                  row, row, vec, vec,
                  pl.BlockSpec(memory_space=pl.ANY)],
        out_specs=[row, row],
        out_shape=[jax.ShapeDtypeStruct((t, d), F32), jax.ShapeDtypeStruct((t, d), BF16)],
        scratch_shapes=[pltpu.VMEM((2, TOP_K, tm, d // 2), jnp.uint32), pltpu.SemaphoreType.DMA((2,))],
        compiler_params=_params("arbitrary"),
        name="combine_ln",
    )(dest, dest, gates_t, shared, h, g.reshape(1, d), b.reshape(1, d), y)


def _table_lookup(table, idx):
    n = table.shape[0]
    hit = idx[None] == jnp.arange(n, dtype=idx.dtype).reshape((n,) + (1,) * idx.ndim)
    return jnp.sum(jnp.where(hit, table.reshape((n,) + (1,) * idx.ndim), 0), axis=0)


def _moe_ffn_ln(h32, h16, routing, layer, w_gu, w_down, w_sh_gu, w_sh_down, ln_g, ln_b, alpha):
    t, d = h32.shape
    blk = EXPERT_BLOCK
    n_blocks = -(-(t * TOP_K) // blk) + N_EXPERTS
    n_rows = n_blocks * blk
    i32 = jnp.int32

    e_k, pos_k, gate_k, counts = routing
    counts = counts[:, 0]
    padded = (counts + blk - 1) // blk * blk
    pad_ends = jnp.cumsum(padded).astype(i32)
    pad_starts = pad_ends - padded
    dest = _table_lookup(pad_starts, e_k) + pos_k
    n_used = (pad_ends[-1:] // blk).astype(i32)
    block_id = jnp.arange(n_blocks, dtype=i32)
    block_expert = jnp.minimum(jnp.sum(pad_ends[None, :] <= (block_id * blk)[:, None], axis=1),
                               N_EXPERTS - 1).astype(i32)
    last_expert = jnp.sum(jnp.where(block_id == n_used[0] - 1, block_expert, 0))
    block_expert = jnp.where(block_id < n_used[0], block_expert, last_expert)
    block_first = jnp.logical_or(block_id == 0, block_expert != jnp.roll(block_expert, 1)).astype(i32)
    expert_id = jnp.arange(N_EXPERTS, dtype=i32)
    owner = jnp.where(padded > 0, expert_id, N_EXPERTS)
    at_or_after = lax.cummin(owner[::-1])[::-1]
    after = jnp.concatenate([at_or_after[1:], jnp.full((1,), N_EXPERTS, i32)])
    first_row = layer * N_EXPERTS
    next_expert = jnp.where(after < N_EXPERTS, after + first_row, -1).astype(i32)
    block_next = _table_lookup(next_expert, block_expert)

    xs = _dispatch(h16, dest, pad_ends, padded.astype(i32), n_used, n_rows)
    y = _grouped_swiglu(xs, w_gu.reshape((-1,) + w_gu.shape[2:]), w_down.reshape((-1,) + w_down.shape[2:]),
                        block_expert + first_row, block_first, block_next, n_used, blk, pack_out=True)

    n_sh = t // blk
    shared = _grouped_swiglu(h16, w_sh_gu, w_sh_down,
                             jnp.full((n_sh,), layer, i32), (jnp.arange(n_sh) == 0).astype(i32),
                             jnp.full((n_sh,), -1, i32), jnp.full((1,), n_sh, i32), blk, pack_out=False)
    return _combine_ln(dest, gate_k.T, shared, h32, y, ln_g, ln_b, alpha)


def kernel(x, w_in_ab, w_pool, pool_scale, sgu_ln_g, sgu_ln_b, sgu_w, sgu_b, w_out_ab, w_pw1, b_pw1, dw_w, dw_b, conv_ln_g, conv_ln_b, w_pw2, b_pw2, mix_ln_g, mix_ln_b, w_router, b_router, w_gu, w_down, w_shared_gu, w_shared_down, ffn_ln_g, ffn_ln_b):
    bsz, seq, d = x.shape
    depth = mix_ln_g.shape[0]
    alpha = (2 * depth) ** 0.25
    h32 = x.reshape(bsz * seq, d)
    h16 = h32.astype(BF16)
    for layer in range(depth):
        i = layer // 2
        if layer % 2 == 0:
            proj = _matmul(h16, w_in_ab[i].astype(BF16))
            ab = _pool_sgu_mix(proj, w_pool[i], pool_scale[i], sgu_ln_g[i], sgu_ln_b[i], sgu_w[i], sgu_b[i], seq)
            mix = _matmul(ab, w_out_ab[i].astype(BF16))
        else:
            g3 = _pointwise_glu(h16, w_pw1[i].astype(BF16), b_pw1[i])
            cb = _conv_ln_swish(g3, dw_w[i], dw_b[i], conv_ln_g[i], conv_ln_b[i], seq)
            mix = _matmul(cb, w_pw2[i].astype(BF16), bias=b_pw2[i])
        h32, h16, *routing = _residual_ln_route(h32, mix, mix_ln_g[layer], mix_ln_b[layer],
                                                w_router[layer], b_router[layer], alpha)
        h32, h16 = _moe_ffn_ln(h32, h16, routing, layer, w_gu, w_down, w_shared_gu, w_shared_down,
                               ffn_ln_g[layer], ffn_ln_b[layer], alpha)
    return h32.reshape(bsz, seq, d)
```
